```python
import math
import jax, jax.numpy as jnp
from jax import lax
import numpy as np

D_MODEL = 1024
BATCH = 8
SEQ = 4096
DEPTH = 2

CHUNK = 64
Q_BLOCK = 128
EPS = 1e-6

RET_HEADS = 4
RET_DK = 128
RET_DV = 128
RET_W = RET_HEADS * RET_DV
SB_HEADS = 8
SB_DH = 64
SB_W = SB_HEADS * SB_DH
ROPE_BASE = 10000.0
EVEN_IN = 4 * RET_W + 3 * SB_W
EVEN_MIX = RET_W + SB_W

CONV_W = 512
CONV_K = 3
LRU_W = 512
LRU_HEADS = 8
LRU_BLK = LRU_W // LRU_HEADS
LRU_CONV_K = 4
LRU_C = 8.0
ODD_IN = 3 * CONV_W + 2 * LRU_W
ODD_MIX = CONV_W + LRU_W

D_FF = 2816
N_EXPERTS = 8
TOP_K = 2
D_EXPERT = 3584
EXPERT_BLOCK = 256

N_EVEN = (DEPTH + 1) // 2
N_ODD = DEPTH // 2

kernel_name = "hybrid_retention_stickbreak_shortconv_rglru_moe"


def rmsnorm(x, g):
    xf = x.astype(jnp.float32)
    y = xf * lax.rsqrt(jnp.mean(xf * xf, axis=-1, keepdims=True) + EPS)
    return y.astype(x.dtype) * g


def rotary(x, pos):
    d = x.shape[-1]
    inv_freq = 1.0 / (ROPE_BASE ** (jnp.arange(0, d, 2, dtype=jnp.float32) / d))
    ang = pos[:, None] * inv_freq[None, :]
    cos = jnp.cos(ang)[None, :, None, :].astype(x.dtype)
    sin = jnp.sin(ang)[None, :, None, :].astype(x.dtype)
    x1, x2 = jnp.split(x, 2, axis=-1)
    return jnp.concatenate([x1 * cos - x2 * sin, x1 * sin + x2 * cos], axis=-1)


def retention(q, k, v):
    B, T, H, dk = q.shape
    dv = v.shape[-1]
    n = T // CHUNK
    log_g = jnp.log(1.0 - 2.0 ** (-5.0 - jnp.arange(H, dtype=jnp.float32)))
    idx = jnp.arange(CHUNK, dtype=jnp.float32)
    intra = jnp.exp(log_g[:, None, None] * jnp.abs(idx[:, None] - idx[None, :]))
    q_decay = jnp.exp(log_g[None, :] * (idx + 1.0)[:, None])
    k_decay = jnp.exp(log_g[None, :] * (CHUNK - 1.0 - idx)[:, None])
    chunk_decay = jnp.exp(log_g * CHUNK)

    qc = q.reshape(B, n, CHUNK, H, dk)
    kc = k.reshape(B, n, CHUNK, H, dk)
    vc = v.reshape(B, n, CHUNK, H, dv)

    s = jnp.einsum('bnihd,bnjhd->bnhij', qc, kc) * intra.astype(q.dtype)
    o_intra = jnp.einsum('bnhij,bnjhe->bnihe', s, vc).astype(jnp.float32)

    kc_dec = kc * k_decay.astype(k.dtype)[None, None, :, :, None]
    kv = jnp.einsum('bnjhd,bnjhe->bnhde', kc_dec, vc).astype(jnp.float32)

    def step(state, kv_n):
        return chunk_decay[None, :, None, None] * state + kv_n, state

    _, prev = lax.scan(step, jnp.zeros((B, H, dk, dv), jnp.float32), jnp.moveaxis(kv, 1, 0))
    prev = jnp.moveaxis(prev, 0, 1)
    o_cross = jnp.einsum('bnihd,bnhde->bnihe', qc.astype(jnp.float32), prev) * q_decay[None, None, :, :, None]
    return (o_intra + o_cross).reshape(B, T, H, dv)


def stick_breaking(q, k, v):
    B, T, H, d = q.shape
    scale = d ** -0.5
    outs = []
    for qb in range(T // Q_BLOCK):
        q0 = qb * Q_BLOCK
        kl = q0 + Q_BLOCK
        z = jnp.einsum('bihd,bjhd->bhij', q[:, q0:kl], k[:, :kl]).astype(jnp.float32) * scale
        t_pos = q0 + jnp.arange(Q_BLOCK)
        mask = jnp.arange(kl)[None, :] < t_pos[:, None]
        log_stay = jnp.where(mask, jax.nn.log_sigmoid(-z), 0.0)
        after = lax.cumsum(log_stay, axis=3, reverse=True) - log_stay
        a = jnp.where(mask, jnp.exp(jax.nn.log_sigmoid(z) + after), 0.0)
        outs.append(jnp.einsum('bhij,bjhd->bihd', a.astype(v.dtype), v[:, :kl]))
    return jnp.concatenate(outs, axis=1)


def causal_depthwise_conv(x, w):
    K, C = w.shape
    return lax.conv_general_dilated(
        x, w[:, None, :].astype(x.dtype), window_strides=(1,), padding=[(K - 1, 0)],
        dimension_numbers=('NWC', 'WIO', 'NWC'), feature_group_count=C)


def _lin_rec_combine(left, right):
    a_l, b_l = left
    a_r, b_r = right
    return a_l * a_r, a_r * b_l + b_r


def rg_lru(xr, conv_w, conv_b, wa, ba, wx, bx, lam):
    B, T, W = xr.shape
    xr = causal_depthwise_conv(xr, conv_w) + conv_b
    xh = xr.reshape(B, T, LRU_HEADS, LRU_BLK)
    r = jax.nn.sigmoid(jnp.einsum('bthi,hij->bthj', xh, wa).reshape(B, T, W) + ba)
    ig = jax.nn.sigmoid(jnp.einsum('bthi,hij->bthj', xh, wx).reshape(B, T, W) + bx)
    log_a = (-LRU_C * r.astype(jnp.float32)) * jax.nn.softplus(-lam.astype(jnp.float32))
    a = jnp.exp(log_a)
    mult = jnp.sqrt(-jnp.expm1(2.0 * log_a))
    b = mult * (ig * xr).astype(jnp.float32)
    _, h = lax.associative_scan(_lin_rec_combine, (a, b), axis=1)
    return h


def even_mixer(h, w_in, gn_w, w_out):
    B, T, _ = h.shape
    proj = h @ w_in
    cuts = [RET_W, 2 * RET_W, 3 * RET_W, 4 * RET_W, 4 * RET_W + SB_W, 4 * RET_W + 2 * SB_W]
    rq, rk, rv, rg, sq, sk, sv = jnp.split(proj, cuts, axis=-1)
    pos = jnp.arange(T, dtype=jnp.float32)
    rq = rotary(rq.reshape(B, T, RET_HEADS, RET_DK), pos)
    rk = rotary(rk.reshape(B, T, RET_HEADS, RET_DK), pos) * (RET_DK ** -0.5)
    o = retention(rq, rk, rv.reshape(B, T, RET_HEADS, RET_DV))
    mu = jnp.mean(o, axis=-1, keepdims=True)
    var = jnp.mean(jnp.square(o - mu), axis=-1, keepdims=True)
    o = ((o - mu) * lax.rsqrt(var + EPS)).reshape(B, T, RET_W)
    y_ret = (o.astype(h.dtype) * gn_w) * jax.nn.silu(rg)
    y_sb = stick_breaking(sq.reshape(B, T, SB_HEADS, SB_DH), sk.reshape(B, T, SB_HEADS, SB_DH),
                          sv.reshape(B, T, SB_HEADS, SB_DH)).reshape(B, T, SB_W)
    return jnp.concatenate([y_ret, y_sb.astype(h.dtype)], axis=-1) @ w_out


def odd_mixer(h, w_in, conv_w, lru_conv_w, lru_conv_b, wa, ba, wx, bx, lam, w_out):
    proj = h @ w_in
    cuts = [CONV_W, 2 * CONV_W, 3 * CONV_W, 3 * CONV_W + LRU_W]
    gb, gc, u, xr, xg = jnp.split(proj, cuts, axis=-1)
    y_conv = gb * causal_depthwise_conv(gc * u, conv_w)
    y_lru = rg_lru(xr, lru_conv_w, lru_conv_b, wa, ba, wx, bx, lam).astype(h.dtype) * jax.nn.gelu(xg)
    return jnp.concatenate([y_conv, y_lru], axis=-1) @ w_out


def swiglu(h, w_gate, w_up, w_down):
    return (jax.nn.silu(h @ w_gate) * (h @ w_up)) @ w_down


def moe_swiglu(h, w_router, b_router, w_gate, w_up, w_down):
    B, T, D = h.shape
    xt = h.reshape(-1, D)
    N = xt.shape[0]
    logits = (xt @ w_router).astype(jnp.float32) + b_router.astype(jnp.float32)
    top_logit, top_idx = lax.top_k(logits, TOP_K)
    top_w = jax.nn.softmax(top_logit, axis=-1)
    flat_e = top_idx.reshape(-1)
    flat_tok = jnp.repeat(jnp.arange(N, dtype=jnp.int32), TOP_K)
    flat_w = top_w.reshape(-1)
    order = jnp.argsort(flat_e)
    sorted_e = flat_e[order]
    counts = jnp.bincount(flat_e, length=N_EXPERTS)
    start = jnp.cumsum(counts) - counts
    padded = (counts + EXPERT_BLOCK - 1) // EXPERT_BLOCK * EXPERT_BLOCK
    pad_end = jnp.cumsum(padded)
    pad_start = pad_end - padded
    rank = jnp.arange(N * TOP_K) - start[sorted_e]
    dest = pad_start[sorted_e] + rank
    n_slots = N * TOP_K + N_EXPERTS * EXPERT_BLOCK
    n_blocks = n_slots // EXPERT_BLOCK
    slot_tok = jnp.zeros((n_slots,), jnp.int32).at[dest].set(flat_tok[order])
    slot_w = jnp.zeros((n_slots,), jnp.float32).at[dest].set(flat_w[order])
    block_e = jnp.minimum(jnp.searchsorted(pad_end, jnp.arange(n_blocks) * EXPERT_BLOCK, side='right'),
                          N_EXPERTS - 1)
    xb = xt[slot_tok].reshape(n_blocks, EXPERT_BLOCK, D)

    def expert_block(args):
        xg, e = args
        return (jax.nn.silu(xg @ w_gate[e]) * (xg @ w_up[e])) @ w_down[e]

    yb = lax.map(expert_block, (xb, block_e)).reshape(n_slots, D)
    y = yb * slot_w[:, None].astype(yb.dtype)
    out = jnp.zeros_like(xt).at[slot_tok].add(y)
    return out.reshape(B, T, D)


def _normal(k, shape, scale):
    return jax.random.normal(k, shape, jnp.float32) * scale


def setup_inputs(seed: int = 0) -> dict:
    key = jax.random.key(seed)
    ks = jax.random.split(key, 32)
    D = D_MODEL
    u = jax.random.uniform(ks[18], (N_ODD, LRU_W), jnp.float32, 0.9, 0.999)
    a0 = u ** (1.0 / LRU_C)
    lam = jnp.log(a0) - jnp.log1p(-a0)
    return {
        "x": _normal(ks[0], (BATCH, SEQ, D), 1.0),
        "norm_mix": 1.0 + _normal(ks[1], (DEPTH, D), 0.02),
        "norm_ffn": 1.0 + _normal(ks[2], (DEPTH, D), 0.02),
        "norm_final": 1.0 + _normal(ks[3], (D,), 0.02),
        "ev_w_in": _normal(ks[4], (N_EVEN, D, EVEN_IN), D ** -0.5),
        "ev_ret_gn": 1.0 + _normal(ks[5], (N_EVEN, RET_W), 0.02),
        "ev_w_out": _normal(ks[6], (N_EVEN, EVEN_MIX, D), EVEN_MIX ** -0.5),
        "ev_ffn_gate": _normal(ks[7], (N_EVEN, D, D_FF), D ** -0.5),
        "ev_ffn_up": _normal(ks[8], (N_EVEN, D, D_FF), D ** -0.5),
        "ev_ffn_down": _normal(ks[9], (N_EVEN, D_FF, D), D_FF ** -0.5),
        "od_w_in": _normal(ks[10], (N_ODD, D, ODD_IN), D ** -0.5),
        "od_conv_w": _normal(ks[11], (N_ODD, CONV_K, CONV_W), CONV_K ** -0.5),
        "od_lru_conv_w": _normal(ks[12], (N_ODD, LRU_CONV_K, LRU_W), LRU_CONV_K ** -0.5),
        "od_lru_conv_b": _normal(ks[13], (N_ODD, LRU_W), 0.01),
        "od_lru_wa": _normal(ks[14], (N_ODD, LRU_HEADS, LRU_BLK, LRU_BLK), LRU_BLK ** -0.5),
        "od_lru_ba": _normal(ks[15], (N_ODD, LRU_W), 0.01),
        "od_lru_wx": _normal(ks[16], (N_ODD, LRU_HEADS, LRU_BLK, LRU_BLK), LRU_BLK ** -0.5),
        "od_lru_bx": _normal(ks[17], (N_ODD, LRU_W), 0.01),
        "od_lru_lambda": lam,
        "od_w_out": _normal(ks[19], (N_ODD, ODD_MIX, D), ODD_MIX ** -0.5),
        "od_router_w": _normal(ks[20], (N_ODD, D, N_EXPERTS), D ** -0.5),
        "od_router_b": _normal(ks[21], (N_ODD, N_EXPERTS), 0.01),
        "od_exp_gate": _normal(ks[22], (N_ODD, N_EXPERTS, D, D_EXPERT), D ** -0.5),
        "od_exp_up": _normal(ks[23], (N_ODD, N_EXPERTS, D, D_EXPERT), D ** -0.5),
        "od_exp_down": _normal(ks[24], (N_ODD, N_EXPERTS, D_EXPERT, D), D_EXPERT ** -0.5),
    }


def reference(x, norm_mix, norm_ffn, norm_final, ev_w_in, ev_ret_gn, ev_w_out,
              ev_ffn_gate, ev_ffn_up, ev_ffn_down, od_w_in, od_conv_w, od_lru_conv_w,
              od_lru_conv_b, od_lru_wa, od_lru_ba, od_lru_wx, od_lru_bx, od_lru_lambda,
              od_w_out, od_router_w, od_router_b, od_exp_gate, od_exp_up, od_exp_down):
    for layer in range(DEPTH):
        j = layer // 2
        h = rmsnorm(x, norm_mix[layer])
        if layer % 2 == 0:
            x = x + even_mixer(h, ev_w_in[j], ev_ret_gn[j], ev_w_out[j])
            h = rmsnorm(x, norm_ffn[layer])
            x = x + swiglu(h, ev_ffn_gate[j], ev_ffn_up[j], ev_ffn_down[j])
        else:
            x = x + odd_mixer(h, od_w_in[j], od_conv_w[j], od_lru_conv_w[j], od_lru_conv_b[j],
                              od_lru_wa[j], od_lru_ba[j], od_lru_wx[j], od_lru_bx[j],
                              od_lru_lambda[j], od_w_out[j])
            h = rmsnorm(x, norm_ffn[layer])
            x = x + moe_swiglu(h, od_router_w[j], od_router_b[j], od_exp_gate[j],
                               od_exp_up[j], od_exp_down[j])
    return rmsnorm(x, norm_final)
```

```python
import functools
import math

import jax
import jax.numpy as jnp
from jax import lax
from jax.experimental import pallas as pl
from jax.experimental.pallas import tpu as pltpu

F32 = jnp.float32
BF16 = jnp.bfloat16

EPS = 1e-6
CHUNK = 64
RET_HEADS = 4
RET_D = 128
SB_PAIRS = 4
SB_DH = 64
ROPE_BASE = 10000.0
CONV_W = 512
LRU_W = 512
LRU_C = 8.0
N_EXPERTS = 8
LANES = 128
VMEM_LIMIT = 56 * 1024 * 1024
SB_DEAD = 104.0


def _cparams(sem):
    return pltpu.CompilerParams(dimension_semantics=sem, vmem_limit_bytes=VMEM_LIMIT)


def _rms(x, g):
    return x * lax.rsqrt(jnp.mean(x * x, axis=-1, keepdims=True) + EPS) * g


def _norm_proj_kernel(x_ref, g_ref, w_ref, o_ref, h_ref):
    @pl.when(pl.program_id(1) == 0)
    def _():
        h_ref[...] = _rms(x_ref[...], g_ref[...]).astype(BF16)

    o_ref[...] = jnp.dot(h_ref[...], w_ref[...], preferred_element_type=F32).astype(o_ref.dtype)


def norm_proj(x, g, w, *, tm, tn):
    n, d = x.shape
    m = w.shape[1]
    return pl.pallas_call(
        _norm_proj_kernel,
        grid=(n // tm, m // tn),
        in_specs=[
            pl.BlockSpec((tm, d), lambda i, j: (i, 0)),
            pl.BlockSpec((1, d), lambda i, j: (0, 0)),
            pl.BlockSpec((d, tn), lambda i, j: (0, j)),
        ],
        out_specs=pl.BlockSpec((tm, tn), lambda i, j: (i, j)),
        out_shape=jax.ShapeDtypeStruct((n, m), BF16),
        scratch_shapes=[pltpu.VMEM((tm, d), BF16)],
        compiler_params=_cparams(("parallel", "arbitrary")),
        name="norm_proj",
    )(x, g.reshape(1, d), w)


def _out_proj_kernel(*refs, n_in):
    x_ref = refs[0]
    y_refs = refs[1:1 + n_in]
    w_refs = refs[1 + n_in:1 + 2 * n_in]
    o_ref = refs[1 + 2 * n_in]
    acc = x_ref[...]
    for y_ref, w_ref in zip(y_refs, w_refs):
        acc = acc + jnp.dot(y_ref[...], w_ref[...], preferred_element_type=F32)
    o_ref[...] = acc


def out_proj(x, ys, ws, *, tm):
    n, d = x.shape
    n_in = len(ys)
    in_specs = [pl.BlockSpec((tm, d), lambda i: (i, 0))]
    in_specs += [pl.BlockSpec((tm, y.shape[1]), lambda i: (i, 0)) for y in ys]
    in_specs += [pl.BlockSpec(w.shape, lambda i: (0, 0)) for w in ws]
    return pl.pallas_call(
        functools.partial(_out_proj_kernel, n_in=n_in),
        grid=(n // tm,),
        in_specs=in_specs,
        out_specs=pl.BlockSpec((tm, d), lambda i: (i, 0)),
        out_shape=jax.ShapeDtypeStruct((n, d), F32),
        compiler_params=_cparams(("parallel",)),
        name="out_proj",
    )(x, *ys, *ws)


def _retention_kernel(q_ref, k_ref, v_ref, g_ref, cos_ref, sin_ref, dmat_ref, qdec_ref, kdec_ref,
                      sdec_ref, gn_ref, o_ref, s_ref):
    @pl.when(pl.program_id(2) == 0)
    def _():
        s_ref[...] = jnp.zeros_like(s_ref)

    cos = cos_ref[...]
    sin = sin_ref[...]
    q = q_ref[...].astype(F32)
    k = k_ref[...].astype(F32)
    half = RET_D // 2
    q = q * cos + pltpu.roll(q, half, 1) * sin
    k = (k * cos + pltpu.roll(k, half, 1) * sin) * (RET_D ** -0.5)
    qb = q.astype(BF16)
    kb = k.astype(BF16)
    v = v_ref[...]

    s = lax.dot_general(qb, kb, (((1,), (1,)), ((), ())), preferred_element_type=F32)
    s = s * dmat_ref[0]
    o = jnp.dot(s.astype(BF16), v, preferred_element_type=F32)
    state = s_ref[...]
    o = o + jnp.dot(qb, state.astype(BF16), preferred_element_type=F32) * qdec_ref[0]

    kd = (k * kdec_ref[0]).astype(BF16)
    kv = lax.dot_general(kd, v, (((0,), (0,)), ((), ())), preferred_element_type=F32)
    s_ref[...] = state * sdec_ref[0] + kv

    mu = jnp.mean(o, axis=-1, keepdims=True)
    oc = o - mu
    var = jnp.mean(oc * oc, axis=-1, keepdims=True)
    on = oc * lax.rsqrt(var + EPS)
    gate = g_ref[...].astype(F32)
    o_ref[...] = (on * gn_ref[0] * (gate * jax.nn.sigmoid(gate))).astype(o_ref.dtype)


def _retention_tables(t_len, blk):
    h = jnp.arange(RET_HEADS, dtype=F32)
    log_g = jnp.log(1.0 - 2.0 ** (-5.0 - h))
    idx = jnp.arange(blk, dtype=F32)
    diff = idx[:, None] - idx[None, :]
    ci = jnp.arange(blk)[:, None] // CHUNK
    cj = jnp.arange(blk)[None, :] // CHUNK
    expo = jnp.where(ci == cj, jnp.abs(diff), diff)
    dmat = jnp.where(cj <= ci, jnp.exp(log_g[:, None, None] * expo[None]), 0.0)
    qdec = jnp.exp(log_g[:, None] * (idx + 1.0)[None, :])
    kdec = jnp.exp(log_g[:, None] * (blk - 1.0 - idx)[None, :])
    sdec = jnp.exp(log_g * blk)
    bc = lambda a: jnp.broadcast_to(a[:, :, None], (RET_HEADS, blk, RET_D))
    sdec = jnp.broadcast_to(sdec[:, None, None], (RET_HEADS, 1, RET_D))
    inv_freq = 1.0 / (ROPE_BASE ** (jnp.arange(0, RET_D, 2, dtype=F32) / RET_D))
    ang = jnp.arange(t_len, dtype=F32)[:, None] * inv_freq[None, :]
    cos = jnp.cos(ang)
    sin = jnp.sin(ang)
    cos2 = jnp.concatenate([cos, cos], axis=-1)
    sin2 = jnp.concatenate([-sin, sin], axis=-1)
    return cos2, sin2, dmat, bc(qdec), bc(kdec), sdec


def retention(proj, gn_w, *, batch, t_len, blk):
    n = proj.shape[0]
    nt = t_len // blk
    cos2, sin2, dmat, qdec, kdec, sdec = _retention_tables(t_len, blk)
    gn = gn_w.reshape(RET_HEADS, 1, RET_D)
    col = lambda off: pl.BlockSpec((blk, RET_D), lambda b, h, t, off=off: (b * nt + t, off + h))
    tab_t = pl.BlockSpec((blk, RET_D), lambda b, h, t: (t, 0))
    per_h = lambda r, c: pl.BlockSpec((1, r, c), lambda b, h, t: (h, 0, 0))
    return pl.pallas_call(
        _retention_kernel,
        grid=(batch, RET_HEADS, nt),
        in_specs=[col(0), col(RET_HEADS), col(2 * RET_HEADS), col(3 * RET_HEADS),
                  tab_t, tab_t, per_h(blk, blk), per_h(blk, RET_D), per_h(blk, RET_D),
                  per_h(1, RET_D), per_h(1, RET_D)],
        out_specs=pl.BlockSpec((blk, RET_D), lambda b, h, t: (b * nt + t, h)),
        out_shape=jax.ShapeDtypeStruct((n, RET_HEADS * RET_D), BF16),
        scratch_shapes=[pltpu.VMEM((RET_D, RET_D), F32)],
        compiler_params=_cparams(("parallel", "parallel", "arbitrary")),
        name="retention",
    )(proj, proj, proj, proj, cos2, sin2, dmat, qdec, kdec, sdec, gn)


def _softplus(z):
    return jnp.maximum(z, 0.0) + jnp.log(1.0 + jnp.exp(-jnp.abs(z)))


def _sb_kernel(q_ref, k_ref, v_ref, tri_ref, o_ref, carry_ref, acc_ref, *, tq):
    qi = pl.program_id(2)
    lane = lax.broadcasted_iota(jnp.int32, (1, LANES), 1)
    lo_lanes = lane < SB_DH
    q = q_ref[...].astype(F32) * (SB_DH ** -0.5)
    qh = (jnp.where(lo_lanes, q, 0.0).astype(BF16), jnp.where(lo_lanes, 0.0, q).astype(BF16))
    tri = tri_ref[...]
    ones = jnp.ones((tq, tq), BF16)

    def tile(kv_start, mask):
        kb = k_ref[pl.ds(kv_start, tq), :]
        vb = v_ref[pl.ds(kv_start, tq), :]
        pv = []
        for hh in range(2):
            z = lax.dot_general(qh[hh], kb, (((1,), (1,)), ((), ())), preferred_element_type=F32)
            sp = _softplus(z)
            if mask is not None:
                sp = jnp.where(mask, sp, 0.0)
            hi = sp.astype(BF16)
            lo = (sp - hi.astype(F32)).astype(BF16)
            incl = (jnp.dot(hi, tri, preferred_element_type=F32)
                    + jnp.dot(lo, tri, preferred_element_type=F32))
            carry = carry_ref[hh]
            a = jnp.exp(z - incl - carry)
            if mask is not None:
                a = jnp.where(mask, a, 0.0)
            pv.append(jnp.dot(a.astype(BF16), vb, preferred_element_type=F32))
            carry_ref[hh] = carry + (jnp.dot(hi, ones, preferred_element_type=F32)
                                     + jnp.dot(lo, ones, preferred_element_type=F32))
        acc_ref[...] += jnp.where(lo_lanes, pv[0], pv[1])

    carry_ref[...] = jnp.zeros_like(carry_ref)
    acc_ref[...] = jnp.zeros_like(acc_ref)
    row = lax.broadcasted_iota(jnp.int32, (tq, tq), 0)
    colm = lax.broadcasted_iota(jnp.int32, (tq, tq), 1)
    tile(pl.multiple_of(qi * tq, tq), colm < row)

    def alive(jj):
        c = jnp.minimum(carry_ref[0, :, 0:LANES], carry_ref[1, :, 0:LANES])
        return jnp.logical_and(jj < qi, jnp.min(c) < SB_DEAD)

    def body(jj):
        tile(pl.multiple_of((qi - 1 - jj) * tq, tq), None)
        return jj + 1

    lax.while_loop(alive, body, jnp.int32(0))
    o_ref[...] = acc_ref[...].astype(o_ref.dtype)


def stick_breaking(proj, *, batch, t_len, tq, col0):
    n = proj.shape[0]
    nq = t_len // tq
    r = jnp.arange(tq)
    tri = (r[:, None] >= r[None, :]).astype(BF16)
    qspec = pl.BlockSpec((tq, LANES), lambda b, p, i: (b * nq + i, col0 + p))
    kspec = pl.BlockSpec((t_len, LANES), lambda b, p, i: (b, col0 + SB_PAIRS + p))
    vspec = pl.BlockSpec((t_len, LANES), lambda b, p, i: (b, col0 + 2 * SB_PAIRS + p))
    return pl.pallas_call(
        functools.partial(_sb_kernel, tq=tq),
        grid=(batch, SB_PAIRS, nq),
        in_specs=[qspec, kspec, vspec, pl.BlockSpec((tq, tq), lambda b, p, i: (0, 0))],
        out_specs=pl.BlockSpec((tq, LANES), lambda b, p, i: (b * nq + i, p)),
        out_shape=jax.ShapeDtypeStruct((n, SB_PAIRS * LANES), BF16),
        scratch_shapes=[pltpu.VMEM((2, tq, tq), F32), pltpu.VMEM((tq, LANES), F32)],
        compiler_params=_cparams(("parallel", "parallel", "arbitrary")),
        name="stick_breaking",
    )(proj, proj, proj, tri)


def _ffn_kernel(x_ref, g_ref, wg_ref, wu_ref, wd_ref, o_ref, h_ref, acc_ref):
    c = pl.program_id(1)

    @pl.when(c == 0)
    def _():
        h_ref[...] = _rms(x_ref[...], g_ref[...]).astype(BF16)
        acc_ref[...] = x_ref[...]

    h = h_ref[...]
    gate = jnp.dot(h, wg_ref[...], preferred_element_type=F32)
    up = jnp.dot(h, wu_ref[...], preferred_element_type=F32)
    act = (gate * jax.nn.sigmoid(gate) * up).astype(BF16)
    acc_ref[...] += jnp.dot(act, wd_ref[...], preferred_element_type=F32)

    @pl.when(c == pl.num_programs(1) - 1)
    def _():
        o_ref[...] = acc_ref[...]


def ffn(x, g, wg, wu, wd, *, tm, tc):
    n, d = x.shape
    f = wg.shape[1]
    return pl.pallas_call(
        _ffn_kernel,
        grid=(n // tm, f // tc),
        in_specs=[
            pl.BlockSpec((tm, d), lambda i, c: (i, 0)),
            pl.BlockSpec((1, d), lambda i, c: (0, 0)),
            pl.BlockSpec((d, tc), lambda i, c: (0, c)),
            pl.BlockSpec((d, tc), lambda i, c: (0, c)),
            pl.BlockSpec((tc, d), lambda i, c: (c, 0)),
        ],
        out_specs=pl.BlockSpec((tm, d), lambda i, c: (i, 0)),
        out_shape=jax.ShapeDtypeStruct((n, d), F32),
        scratch_shapes=[pltpu.VMEM((tm, d), BF16), pltpu.VMEM((tm, d), F32)],
        compiler_params=_cparams(("parallel", "arbitrary")),
        name="ffn",
    )(x, g.reshape(1, d), wg, wu, wd)


def _shift_rows(cur, tail, d, row8):
    rolled = pltpu.roll(cur, d, 0)
    top = jnp.where(row8 < d, pltpu.roll(tail, d, 0), rolled[0:8])
    return jnp.concatenate([top, rolled[8:]], axis=0)


def _lru_kernel(p_ref, w3_ref, w4_ref, b4_ref, wa_ref, ba_ref, wx_ref, bx_ref, lam_ref, o_ref,
                cu_tail, xr_tail, h_prev, *, blk):
    @pl.when(pl.program_id(1) == 0)
    def _():
        cu_tail[...] = jnp.zeros_like(cu_tail)
        xr_tail[...] = jnp.zeros_like(xr_tail)
        h_prev[...] = jnp.zeros_like(h_prev)

    w = CONV_W
    row8 = lax.broadcasted_iota(jnp.int32, (8, w), 0)
    gb = p_ref[:, 0:w].astype(F32)
    cu = p_ref[:, w:2 * w].astype(F32) * p_ref[:, 2 * w:3 * w].astype(F32)
    xr = p_ref[:, 3 * w:4 * w].astype(F32)
    xg = p_ref[:, 4 * w:5 * w].astype(F32)

    ct = cu_tail[...]
    conv3 = (w3_ref[2:3, :] * cu + w3_ref[1:2, :] * _shift_rows(cu, ct, 1, row8)
             + w3_ref[0:1, :] * _shift_rows(cu, ct, 2, row8))
    cu_tail[...] = cu[blk - 8:blk]
    o_ref[:, 0:w] = (gb * conv3).astype(o_ref.dtype)

    xt = xr_tail[...]
    xc = (w4_ref[3:4, :] * xr + w4_ref[2:3, :] * _shift_rows(xr, xt, 1, row8)
          + w4_ref[1:2, :] * _shift_rows(xr, xt, 2, row8)
          + w4_ref[0:1, :] * _shift_rows(xr, xt, 3, row8)) + b4_ref[...]
    xr_tail[...] = xr[blk - 8:blk]

    xcb = xc.astype(BF16)
    r = jax.nn.sigmoid(jnp.dot(xcb, wa_ref[...], preferred_element_type=F32) + ba_ref[...])
    ig = jax.nn.sigmoid(jnp.dot(xcb, wx_ref[...], preferred_element_type=F32) + bx_ref[...])
    log_a = (-LRU_C * r) * _softplus(-lam_ref[...])
    a = jnp.exp(log_a)
    b = jnp.sqrt(1.0 - a * a) * (ig * xc)

    row = lax.broadcasted_iota(jnp.int32, (blk, w), 0)
    d = 1
    while d < blk:
        keep = row >= d
        a_s = jnp.where(keep, pltpu.roll(a, d, 0), 1.0)
        b_s = jnp.where(keep, pltpu.roll(b, d, 0), 0.0)
        b = a * b_s + b
        a = a * a_s
        d *= 2
    h = a * h_prev[...] + b
    h_prev[...] = h[blk - 1:blk]

    c0 = math.sqrt(2.0 / math.pi)
    gelu = 0.5 * xg * (1.0 + jnp.tanh(c0 * (xg + 0.044715 * (xg * xg * xg))))
    o_ref[:, w:2 * w] = (h * gelu).astype(o_ref.dtype)


def _block_diag(wh):
    nh, bi, bo = wh.shape
    eye = jnp.eye(nh, dtype=wh.dtype)
    return (eye[:, None, :, None] * wh[:, :, None, :]).reshape(nh * bi, nh * bo)


def lru_mixer(proj, w3, w4, b4, wa, ba, wx, bx, lam, *, batch, t_len, blk):
    n, pw = proj.shape
    nt = t_len // blk
    w = CONV_W
    row = lambda a: a.reshape(1, w)
    full = lambda shp: pl.BlockSpec(shp, lambda b, t: (0, 0))
    return pl.pallas_call(
        functools.partial(_lru_kernel, blk=blk),
        grid=(batch, nt),
        in_specs=[pl.BlockSpec((blk, pw), lambda b, t: (b * nt + t, 0)),
                  full((8, w)), full((8, w)), full((1, w)),
                  full((w, w)), full((1, w)), full((w, w)), full((1, w)), full((1, w))],
        out_specs=pl.BlockSpec((blk, 2 * w), lambda b, t: (b * nt + t, 0)),
        out_shape=jax.ShapeDtypeStruct((n, 2 * w), BF16),
        scratch_shapes=[pltpu.VMEM((8, w), F32), pltpu.VMEM((8, w), F32), pltpu.VMEM((1, w), F32)],
        compiler_params=_cparams(("parallel", "arbitrary")),
        name="lru_mixer",
    )(proj, jnp.pad(w3, ((0, 8 - w3.shape[0]), (0, 0))), jnp.pad(w4, ((0, 8 - w4.shape[0]), (0, 0))),
      row(b4), _block_diag(wa).astype(BF16), row(ba), _block_diag(wx).astype(BF16), row(bx), row(lam))


def _router_kernel(x_ref, g_ref, wh_ref, wl_ref, b_ref, h_ref, r_ref):
    h = _rms(x_ref[...], g_ref[...])
    h_ref[...] = h
    hh = h.astype(BF16)
    hl = (h - hh.astype(F32)).astype(BF16)
    wh = wh_ref[...]
    logits = (jnp.dot(hh, wh, preferred_element_type=F32) + jnp.dot(hl, wh, preferred_element_type=F32)
              + jnp.dot(hh, wl_ref[...], preferred_element_type=F32)) + b_ref[...]
    lane = lax.broadcasted_iota(jnp.int32, logits.shape, 1).astype(F32)
    neg = jnp.float32(-jnp.inf)
    logits = jnp.where(lane < N_EXPERTS, logits, neg)
    m0 = jnp.max(logits, axis=-1, keepdims=True)
    i0 = jnp.min(jnp.where(logits == m0, lane, float(LANES)), axis=-1, keepdims=True)
    rest = jnp.where(lane == i0, neg, logits)
    m1 = jnp.max(rest, axis=-1, keepdims=True)
    i1 = jnp.min(jnp.where(rest == m1, lane, float(LANES)), axis=-1, keepdims=True)
    e1 = jnp.exp(m1 - m0)
    w0 = 1.0 / (1.0 + e1)
    w1 = e1 * w0
    out = jnp.where(lane == 0, i0, 0.0)
    out = jnp.where(lane == 1, i1, out)
    out = jnp.where(lane == 2, w0, out)
    out = jnp.where(lane == 3, w1, out)
    r_ref[...] = out


def router(x, g, w_router, b_router, *, tm):
    n, d = x.shape
    wp = jnp.pad(w_router, ((0, 0), (0, LANES - N_EXPERTS)))
    wh = wp.astype(BF16)
    wl = (wp - wh.astype(F32)).astype(BF16)
    bp = jnp.pad(b_router, (0, LANES - N_EXPERTS)).reshape(1, LANES)
    return pl.pallas_call(
        _router_kernel,
        grid=(n // tm,),
        in_specs=[pl.BlockSpec((tm, d), lambda i: (i, 0)), pl.BlockSpec((1, d), lambda i: (0, 0)),
                  pl.BlockSpec((d, LANES), lambda i: (0, 0)), pl.BlockSpec((d, LANES), lambda i: (0, 0)),
                  pl.BlockSpec((1, LANES), lambda i: (0, 0))],
        out_specs=[pl.BlockSpec((tm, d), lambda i: (i, 0)), pl.BlockSpec((tm, LANES), lambda i: (i, 0))],
        out_shape=[jax.ShapeDtypeStruct((n, d), F32), jax.ShapeDtypeStruct((n, LANES), F32)],
        compiler_params=_cparams(("parallel",)),
        name="router",
    )(x, g.reshape(1, d), wh, wl, bp)


def _moe_kernel(be_ref, nv_ref, idx_hbm, sw_ref, h_hbm, wg_hbm, wu_hbm, wd_hbm, o_hbm,
                idx_s, xbuf, ybuf, wg, wu, wd, sem_i, sem_g, sem_s, sem_w, *, tb, fc):
    i = pl.program_id(0)
    nb = pl.num_programs(0)

    def idx_copy(blk):
        return pltpu.make_async_copy(idx_hbm.at[pl.ds(blk * tb, tb)], idx_s.at[blk % 3], sem_i.at[blk % 3])

    def gather_row(blk, r):
        a = jnp.maximum(idx_s[blk % 3, r], 0)
        tok = lax.shift_right_logical(a, 1)
        return pltpu.make_async_copy(h_hbm.at[tok], xbuf.at[blk % 2, r], sem_g.at[blk % 2])

    def start_gather(blk):
        def one(r, c):
            gather_row(blk, r).start()
            return c
        lax.fori_loop(0, tb, one, 0, unroll=8)

    def scatter_row(r, a):
        return pltpu.make_async_copy(ybuf.at[r], o_hbm.at[a], sem_s.at[0])

    def wait_scatter(count):
        def one(r, c):
            scatter_row(0, 0).wait()
            return c
        lax.fori_loop(0, count, one, 0)

    @pl.when(i == 0)
    def _():
        idx_copy(0).start()
        idx_copy(0).wait()

        @pl.when(nb > 1)
        def _():
            idx_copy(1).start()
        start_gather(0)

    e = be_ref[i]
    changed = jnp.logical_or(i == 0, e != be_ref[jnp.maximum(i - 1, 0)])

    @pl.when(changed)
    def _():
        copies = [pltpu.make_async_copy(src.at[e], dst, sem_w.at[j])
                  for j, (src, dst) in enumerate(((wg_hbm, wg), (wu_hbm, wu), (wd_hbm, wd)))]
        for c in copies:
            c.start()
        for c in copies:
            c.wait()

    @pl.when(i + 1 < nb)
    def _():
        idx_copy(i + 1).wait()
        start_gather(i + 1)

        @pl.when(i + 2 < nb)
        def _():
            idx_copy(i + 2).start()

    def wait_one(r, c):
        gather_row(i, 0).wait()
        return c
    lax.fori_loop(0, tb, wait_one, 0)

    x = xbuf[i % 2].astype(BF16)
    d_exp = wg.shape[1]
    acc = jnp.zeros(ybuf.shape, F32)
    for c0 in range(0, d_exp, fc):
        gate = jnp.dot(x, wg[:, c0:c0 + fc], preferred_element_type=F32)
        up = jnp.dot(x, wu[:, c0:c0 + fc], preferred_element_type=F32)
        act = (gate * jax.nn.sigmoid(gate) * up).astype(BF16)
        acc = acc + jnp.dot(act, wd[c0:c0 + fc, :], preferred_element_type=F32)
    y = acc * sw_ref[:, 0:1]

    @pl.when(i > 0)
    def _():
        wait_scatter(nv_ref[jnp.maximum(i - 1, 0)])

    ybuf[...] = y
    nv = nv_ref[i]

    def scat(r, c):
        scatter_row(r, idx_s[i % 3, r]).start()
        return c
    lax.fori_loop(0, nv, scat, 0)

    @pl.when(i == nb - 1)
    def _():
        wait_scatter(nv)


def moe_experts(h, slot_a, slot_w, block_e, n_valid, wg, wu, wd, *, tb, fc):
    n, d = h.shape
    n_slots = slot_a.shape[0]
    nb = n_slots // tb
    f = wg.shape[2]
    grid_spec = pltpu.PrefetchScalarGridSpec(
        num_scalar_prefetch=2,
        grid=(nb,),
        in_specs=[pl.BlockSpec(memory_space=pl.ANY),
                  pl.BlockSpec((tb, LANES), lambda i, be, nv: (i, 0)),
                  pl.BlockSpec(memory_space=pl.ANY), pl.BlockSpec(memory_space=pl.ANY),
                  pl.BlockSpec(memory_space=pl.ANY), pl.BlockSpec(memory_space=pl.ANY)],
        out_specs=pl.BlockSpec(memory_space=pl.ANY),
        scratch_shapes=[pltpu.SMEM((3, tb), jnp.int32),
                        pltpu.VMEM((2, tb, d), F32), pltpu.VMEM((tb, d), F32),
                        pltpu.VMEM((d, f), BF16), pltpu.VMEM((d, f), BF16), pltpu.VMEM((f, d), BF16),
                        pltpu.SemaphoreType.DMA((3,)), pltpu.SemaphoreType.DMA((2,)),
                        pltpu.SemaphoreType.DMA((1,)), pltpu.SemaphoreType.DMA((3,))],
    )
    return pl.pallas_call(
        functools.partial(_moe_kernel, tb=tb, fc=fc),
        grid_spec=grid_spec,
        out_shape=jax.ShapeDtypeStruct((2 * n, d), F32),
        compiler_params=_cparams(("arbitrary",)),
        name="moe_experts",
    )(block_e, n_valid, slot_a, jnp.broadcast_to(slot_w[:, None], (n_slots, LANES)), h, wg, wu, wd)


def _route_slots(top_idx, top_w, *, tb):
    n = top_idx.shape[0]
    flat_e = top_idx.reshape(-1)
    onehot = (flat_e[:, None] == jnp.arange(N_EXPERTS)[None, :]).astype(jnp.int32)
    csum = jnp.cumsum(onehot, axis=0)
    counts = csum[-1]
    rank = jnp.take_along_axis(csum, flat_e[:, None], axis=1)[:, 0] - 1
    padded = (counts + tb - 1) // tb * tb
    pad_end = jnp.cumsum(padded)
    pad_start = pad_end - padded
    dest = pad_start[flat_e] + rank
    n_slots = 2 * n + N_EXPERTS * tb
    nb = n_slots // tb
    slot_a = jnp.full((n_slots,), -1, jnp.int32).at[dest].set(jnp.arange(2 * n, dtype=jnp.int32))
    slot_w = jnp.zeros((n_slots,), F32).at[dest].set(top_w.reshape(-1))
    blk_start = jnp.arange(nb, dtype=jnp.int32) * tb
    block_e = jnp.minimum(jnp.searchsorted(pad_end, blk_start, side="right"), N_EXPERTS - 1).astype(jnp.int32)
    used_end = pad_start + counts
    n_valid = jnp.clip(used_end[block_e] - blk_start, 0, tb).astype(jnp.int32)
    return slot_a, slot_w, block_e, n_valid


def _final_kernel(x_ref, y_ref, g_ref, o_ref):
    d = x_ref.shape[1]
    x = x_ref[...] + y_ref[:, 0:d] + y_ref[:, d:2 * d]
    o_ref[...] = _rms(x, g_ref[...])


def final_norm(x, y2, g, *, tm):
    n, d = x.shape
    return pl.pallas_call(
        _final_kernel,
        grid=(n // tm,),
        in_specs=[pl.BlockSpec((tm, d), lambda i: (i, 0)), pl.BlockSpec((tm, 2 * d), lambda i: (i, 0)),
                  pl.BlockSpec((1, d), lambda i: (0, 0))],
        out_specs=pl.BlockSpec((tm, d), lambda i: (i, 0)),
        out_shape=jax.ShapeDtypeStruct((n, d), F32),
        compiler_params=_cparams(("parallel",)),
        name="final_norm",
    )(x, y2, g.reshape(1, d))


def even_layer(x, g_mix, g_ffn, w_in, gn_w, w_out, wg, wu, wd, *, batch, t_len):
    proj = norm_proj(x, g_mix, w_in.astype(BF16), tm=512, tn=1792)
    y_ret = retention(proj, gn_w, batch=batch, t_len=t_len, blk=256)
    y_sb = stick_breaking(proj, batch=batch, t_len=t_len, tq=256, col0=4 * RET_HEADS)
    half = y_ret.shape[1]
    wo = w_out.astype(BF16)
    x = out_proj(x, [y_ret, y_sb], [wo[:half], wo[half:]], tm=512)
    return ffn(x, g_ffn, wg.astype(BF16), wu.astype(BF16), wd.astype(BF16), tm=512, tc=1408)


def odd_layer_mixer(x, g_mix, w_in, conv_w, lru_conv_w, lru_conv_b, wa, ba, wx, bx, lam, w_out,
                    *, batch, t_len):
    proj = norm_proj(x, g_mix, w_in.astype(BF16), tm=512, tn=1280)
    y = lru_mixer(proj, conv_w, lru_conv_w, lru_conv_b, wa, ba, wx, bx, lam,
                  batch=batch, t_len=t_len, blk=256)
    return out_proj(x, [y], [w_out.astype(BF16)], tm=512)


def moe_and_final(x, g_ffn, g_final, w_router, b_router, wg, wu, wd, *, tb=256):
    h, r = router(x, g_ffn, w_router, b_router, tm=512)
    top_idx = r[:, 0:2].astype(jnp.int32)
    top_w = r[:, 2:4]
    slot_a, slot_w, block_e, n_valid = _route_slots(top_idx, top_w, tb=tb)
    y2 = moe_experts(h, slot_a, slot_w, block_e, n_valid,
                     wg.astype(BF16), wu.astype(BF16), wd.astype(BF16), tb=tb, fc=512)
    return final_norm(x, y2.reshape(x.shape[0], 2 * x.shape[1]), g_final, tm=512)


def kernel(x, norm_mix, norm_ffn, norm_final, ev_w_in, ev_ret_gn, ev_w_out, ev_ffn_gate, ev_ffn_up,
           ev_ffn_down, od_w_in, od_conv_w, od_lru_conv_w, od_lru_conv_b, od_lru_wa, od_lru_ba,
           od_lru_wx, od_lru_bx, od_lru_lambda, od_w_out, od_router_w, od_router_b, od_exp_gate,
           od_exp_up, od_exp_down):
    batch, t_len, d = x.shape
    xf = x.reshape(batch * t_len, d)
    xf = even_layer(xf, norm_mix[0], norm_ffn[0], ev_w_in[0], ev_ret_gn[0], ev_w_out[0],
                    ev_ffn_gate[0], ev_ffn_up[0], ev_ffn_down[0], batch=batch, t_len=t_len)
    xf = odd_layer_mixer(xf, norm_mix[1], od_w_in[0], od_conv_w[0], od_lru_conv_w[0], od_lru_conv_b[0],
                         od_lru_wa[0], od_lru_ba[0], od_lru_wx[0], od_lru_bx[0], od_lru_lambda[0],
                         od_w_out[0], batch=batch, t_len=t_len)
    out = moe_and_final(xf, norm_ffn[1], norm_final, od_router_w[0], od_router_b[0],
                        od_exp_gate[0], od_exp_up[0], od_exp_down[0])
    return out.reshape(batch, t_len, d)
```

```python
import functools
import math

import jax
import jax.numpy as jnp
from jax import lax
from jax.experimental import pallas as pl
from jax.experimental.pallas import tpu as pltpu

F32 = jnp.float32
BF16 = jnp.bfloat16

EPS = 1e-6
CHUNK = 64
RET_HEADS = 4
RET_D = 128
SB_PAIRS = 4
SB_DH = 64
ROPE_BASE = 10000.0
CONV_W = 512
LRU_W = 512
LRU_C = 8.0
N_EXPERTS = 8
LANES = 128
VMEM_LIMIT = 56 * 1024 * 1024
SB_DEAD = 104.0


def _cparams(sem):
    return pltpu.CompilerParams(dimension_semantics=sem, vmem_limit_bytes=VMEM_LIMIT)


def _rms(x, g):
    return x * lax.rsqrt(jnp.mean(x * x, axis=-1, keepdims=True) + EPS) * g


def _for_slot(slot, fn):
    for s in range(2):
        pl.when(slot == s)(functools.partial(fn, s))


def _norm_proj_kernel(x_ref, g_ref, w_ref, o_ref):
    h = _rms(x_ref[...], g_ref[...]).astype(BF16)
    o_ref[...] = jnp.dot(h, w_ref[...], preferred_element_type=F32).astype(o_ref.dtype)


def norm_proj(x, g, w, *, tm):
    n, d = x.shape
    m = w.shape[1]
    return pl.pallas_call(
        _norm_proj_kernel,
        grid=(n // tm,),
        in_specs=[
            pl.BlockSpec((tm, d), lambda i: (i, 0)),
            pl.BlockSpec((1, d), lambda i: (0, 0)),
            pl.BlockSpec((d, m), lambda i: (0, 0)),
        ],
        out_specs=pl.BlockSpec((tm, m), lambda i: (i, 0)),
        out_shape=jax.ShapeDtypeStruct((n, m), BF16),
        compiler_params=_cparams(("parallel",)),
        name="norm_proj",
    )(x, g.reshape(1, d), w)


def _retention_kernel(q_ref, k_ref, v_ref, g_ref, cos_ref, sin_ref, dmat_ref, qdec_ref, kdec_ref,
                      sdec_ref, gn_ref, o_ref, s_ref):
    @pl.when(pl.program_id(1) == 0)
    def _():
        s_ref[...] = jnp.zeros_like(s_ref)

    cos = cos_ref[...]
    sin = sin_ref[...]
    half = RET_D // 2
    for h in range(RET_HEADS):
        cols = slice(h * RET_D, (h + 1) * RET_D)
        q = q_ref[:, cols].astype(F32)
        k = k_ref[:, cols].astype(F32)
        q = q * cos + pltpu.roll(q, half, 1) * sin
        k = (k * cos + pltpu.roll(k, half, 1) * sin) * (RET_D ** -0.5)
        qb = q.astype(BF16)
        kb = k.astype(BF16)
        v = v_ref[:, cols]

        s = lax.dot_general(qb, kb, (((1,), (1,)), ((), ())), preferred_element_type=F32)
        s = s * dmat_ref[h]
        o = jnp.dot(s.astype(BF16), v, preferred_element_type=F32)
        state = s_ref[h]
        o = o + jnp.dot(qb, state.astype(BF16), preferred_element_type=F32) * qdec_ref[h]

        kd = (k * kdec_ref[h]).astype(BF16)
        kv = lax.dot_general(kd, v, (((0,), (0,)), ((), ())), preferred_element_type=F32)
        s_ref[h] = state * sdec_ref[h] + kv

        mu = jnp.mean(o, axis=-1, keepdims=True)
        oc = o - mu
        var = jnp.mean(oc * oc, axis=-1, keepdims=True)
        on = oc * lax.rsqrt(var + EPS)
        gate = g_ref[:, cols].astype(F32)
        o_ref[:, cols] = (on * gn_ref[h] * (gate * jax.nn.sigmoid(gate))).astype(o_ref.dtype)


def _retention_tables(t_len, blk):
    h = jnp.arange(RET_HEADS, dtype=F32)
    log_g = jnp.log(1.0 - 2.0 ** (-5.0 - h))
    idx = jnp.arange(blk, dtype=F32)
    diff = idx[:, None] - idx[None, :]
    ci = jnp.arange(blk)[:, None] // CHUNK
    cj = jnp.arange(blk)[None, :] // CHUNK
    expo = jnp.where(ci == cj, jnp.abs(diff), diff)
    dmat = jnp.where(cj <= ci, jnp.exp(log_g[:, None, None] * expo[None]), 0.0)
    qdec = jnp.exp(log_g[:, None] * (idx + 1.0)[None, :])
    kdec = jnp.exp(log_g[:, None] * (blk - 1.0 - idx)[None, :])
    sdec = jnp.exp(log_g * blk)
    bc = lambda a: jnp.broadcast_to(a[:, :, None], (RET_HEADS, blk, RET_D))
    sdec = jnp.broadcast_to(sdec[:, None, None], (RET_HEADS, 1, RET_D))
    inv_freq = 1.0 / (ROPE_BASE ** (jnp.arange(0, RET_D, 2, dtype=F32) / RET_D))
    ang = jnp.arange(t_len, dtype=F32)[:, None] * inv_freq[None, :]
    cos = jnp.cos(ang)
    sin = jnp.sin(ang)
    cos2 = jnp.concatenate([cos, cos], axis=-1)
    sin2 = jnp.concatenate([-sin, sin], axis=-1)
    return cos2, sin2, dmat, bc(qdec), bc(kdec), sdec


def retention(proj, gn_w, *, batch, t_len, blk):
    n = proj.shape[0]
    nt = t_len // blk
    w = RET_HEADS * RET_D
    cos2, sin2, dmat, qdec, kdec, sdec = _retention_tables(t_len, blk)
    gn = gn_w.reshape(RET_HEADS, 1, RET_D)
    col = lambda c: pl.BlockSpec((blk, w), lambda b, t, c=c: (b * nt + t, c))
    tab_t = pl.BlockSpec((blk, RET_D), lambda b, t: (t, 0))
    whole = lambda a: pl.BlockSpec(a.shape, lambda b, t: (0, 0, 0))
    return pl.pallas_call(
        _retention_kernel,
        grid=(batch, nt),
        in_specs=[col(0), col(1), col(2), col(3), tab_t, tab_t,
                  whole(dmat), whole(qdec), whole(kdec), whole(sdec), whole(gn)],
        out_specs=pl.BlockSpec((blk, w), lambda b, t: (b * nt + t, 0)),
        out_shape=jax.ShapeDtypeStruct((n, w), BF16),
        scratch_shapes=[pltpu.VMEM((RET_HEADS, RET_D, RET_D), F32)],
        compiler_params=_cparams(("parallel", "arbitrary")),
        name="retention",
    )(proj, proj, proj, proj, cos2, sin2, dmat, qdec, kdec, sdec, gn)


def _softplus(z):
    return jnp.maximum(z, 0.0) + jnp.log(1.0 + jnp.exp(-jnp.abs(z)))


def _sb_kernel(q_ref, k_ref, v_ref, tri_ref, o_ref, carry_ref, acc_ref, qs_ref, *, tq):
    qi = pl.program_id(1)
    lane = lax.broadcasted_iota(jnp.int32, (1, LANES), 1)
    lo_lanes = lane < SB_DH
    tri2 = tri_ref[...]

    n_heads = 2 * SB_PAIRS
    for p in range(SB_PAIRS):
        q = q_ref[:, p * LANES:(p + 1) * LANES].astype(F32) * (SB_DH ** -0.5)
        qs_ref[(2 * p) * tq:(2 * p + 1) * tq, :] = jnp.where(lo_lanes, q, 0.0).astype(BF16)
        qs_ref[(2 * p + 1) * tq:(2 * p + 2) * tq, :] = jnp.where(lo_lanes, 0.0, q).astype(BF16)

    def tile(kv_start, mask):
        z = jnp.concatenate([
            lax.dot_general(qs_ref[2 * p * tq:(2 * p + 2) * tq, :], k_ref[pl.ds(kv_start, tq), p * LANES:(p + 1) * LANES],
                            (((1,), (1,)), ((), ())), preferred_element_type=F32)
            for p in range(SB_PAIRS)], axis=0)
        sp = _softplus(z)
        if mask is not None:
            mask = jnp.concatenate([mask] * n_heads, axis=0)
            sp = jnp.where(mask, sp, 0.0)
        hi = sp.astype(BF16)
        lo = (sp - hi.astype(F32)).astype(BF16)
        sums = jnp.dot(jnp.concatenate([hi, lo], axis=1), tri2, preferred_element_type=F32)
        carry = carry_ref[...]
        new_carry = carry + jnp.broadcast_to(sums[:, 0:1], carry.shape)
        carry_ref[...] = new_carry
        a = jnp.exp(z - sums - jnp.concatenate([carry] * (tq // LANES), axis=1))
        if mask is not None:
            a = jnp.where(mask, a, 0.0)
        a = a.astype(BF16)
        for p in range(SB_PAIRS):
            cols = slice(p * LANES, (p + 1) * LANES)
            pv = jnp.dot(a[2 * p * tq:(2 * p + 2) * tq, :], v_ref[pl.ds(kv_start, tq), cols],
                         preferred_element_type=F32)
            acc_ref[:, cols] += jnp.where(lo_lanes, pv[0:tq], pv[tq:2 * tq])
        return jnp.min(new_carry)

    carry_ref[...] = jnp.zeros_like(carry_ref)
    acc_ref[...] = jnp.zeros_like(acc_ref)
    row = lax.broadcasted_iota(jnp.int32, (tq, tq), 0)
    colm = lax.broadcasted_iota(jnp.int32, (tq, tq), 1)
    least0 = tile(pl.multiple_of(qi * tq, tq), colm < row)

    def alive(state):
        jj, least = state
        return jnp.logical_and(jj < qi, least < SB_DEAD)

    def body(state):
        jj, _ = state
        return jj + 1, tile(pl.multiple_of((qi - 1 - jj) * tq, tq), None)

    lax.while_loop(alive, body, (jnp.int32(0), least0))
    o_ref[...] = acc_ref[...].astype(o_ref.dtype)


def stick_breaking(proj, *, batch, t_len, tq, col0):
    n = proj.shape[0]
    nq = t_len // tq
    w = SB_PAIRS * LANES
    r = jnp.arange(tq)
    tri = (r[:, None] >= r[None, :]).astype(BF16)
    tri2 = jnp.concatenate([tri, tri], axis=0)
    return pl.pallas_call(
        functools.partial(_sb_kernel, tq=tq),
        grid=(batch, nq),
        in_specs=[pl.BlockSpec((tq, w), lambda b, i: (b * nq + i, col0)),
                  pl.BlockSpec((t_len, w), lambda b, i: (b, col0 + 1)),
                  pl.BlockSpec((t_len, w), lambda b, i: (b, col0 + 2)),
                  pl.BlockSpec(tri2.shape, lambda b, i: (0, 0))],
        out_specs=pl.BlockSpec((tq, w), lambda b, i: (b * nq + i, 0)),
        out_shape=jax.ShapeDtypeStruct((n, w), BF16),
        scratch_shapes=[pltpu.VMEM((2 * SB_PAIRS * tq, LANES), F32), pltpu.VMEM((tq, w), F32),
                        pltpu.VMEM((2 * SB_PAIRS * tq, LANES), BF16)],
        compiler_params=_cparams(("parallel", "arbitrary")),
        name="stick_breaking",
    )(proj, proj, proj, tri2)


def _load_once(pairs, sem):
    @pl.when(pl.program_id(0) == 0)
    def _():
        copies = [pltpu.make_async_copy(src, dst, sem.at[j]) for j, (src, dst) in enumerate(pairs)]
        for c in copies:
            c.start()
        for c in copies:
            c.wait()


def _ffn_kernel(*refs, n_in, fc):
    x_ref, g_ref = refs[0], refs[1]
    y_refs = refs[2:2 + n_in]
    wo_refs = refs[2 + n_in:2 + 2 * n_in]
    wg_hbm, wu_hbm, wd_hbm, o_ref, wg, wu, wd, sem = refs[2 + 2 * n_in:]
    _load_once(((wg_hbm, wg), (wu_hbm, wu), (wd_hbm, wd)), sem)

    x1 = x_ref[...]
    for y_ref, wo_ref in zip(y_refs, wo_refs):
        x1 = x1 + jnp.dot(y_ref[...], wo_ref[...], preferred_element_type=F32)
    h = _rms(x1, g_ref[...]).astype(BF16)
    ff = jnp.zeros_like(x1)
    for c0 in range(0, wg.shape[1], fc):
        gate = jnp.dot(h, wg[:, c0:c0 + fc], preferred_element_type=F32)
        up = jnp.dot(h, wu[:, c0:c0 + fc], preferred_element_type=F32)
        act = (gate * jax.nn.sigmoid(gate) * up).astype(BF16)
        ff = ff + jnp.dot(act, wd[c0:c0 + fc, :], preferred_element_type=F32)
    o_ref[...] = x1 + ff


def ffn(x, g, ys, wos, wg, wu, wd, *, tm, fc):
    n, d = x.shape
    f = wg.shape[1]
    n_in = len(ys)
    hbm = pl.BlockSpec(memory_space=pl.ANY)
    in_specs = [pl.BlockSpec((tm, d), lambda i: (i, 0)), pl.BlockSpec((1, d), lambda i: (0, 0))]
    in_specs += [pl.BlockSpec((tm, y.shape[1]), lambda i: (i, 0)) for y in ys]
    in_specs += [pl.BlockSpec(w.shape, lambda i: (0, 0)) for w in wos]
    in_specs += [hbm, hbm, hbm]
    return pl.pallas_call(
        functools.partial(_ffn_kernel, n_in=n_in, fc=fc),
        grid=(n // tm,),
        in_specs=in_specs,
        out_specs=pl.BlockSpec((tm, d), lambda i: (i, 0)),
        out_shape=jax.ShapeDtypeStruct((n, d), F32),
        scratch_shapes=[pltpu.VMEM((d, f), BF16), pltpu.VMEM((d, f), BF16), pltpu.VMEM((f, d), BF16),
                        pltpu.SemaphoreType.DMA((3,))],
        compiler_params=_cparams(("arbitrary",)),
        name="ffn",
    )(x, g.reshape(1, d), *ys, *wos, wg, wu, wd)


def _shift_rows(cur, tail, d, row8):
    rolled = pltpu.roll(cur, d, 0)
    top = jnp.where(row8 < d, pltpu.roll(tail, d, 0), rolled[0:8])
    return jnp.concatenate([top, rolled[8:]], axis=0)


def _lru_kernel(p_ref, w3_ref, w4_ref, b4_ref, wa_ref, ba_ref, wx_ref, bx_ref, lam_ref, o_ref,
                cu_tail, xr_tail, h_prev, *, blk):
    @pl.when(pl.program_id(1) == 0)
    def _():
        cu_tail[...] = jnp.zeros_like(cu_tail)
        xr_tail[...] = jnp.zeros_like(xr_tail)
        h_prev[...] = jnp.zeros_like(h_prev)

    w = CONV_W
    row8 = lax.broadcasted_iota(jnp.int32, (8, w), 0)
    gb = p_ref[:, 0:w].astype(F32)
    cu = p_ref[:, w:2 * w].astype(F32) * p_ref[:, 2 * w:3 * w].astype(F32)
    xr = p_ref[:, 3 * w:4 * w].astype(F32)
    xg = p_ref[:, 4 * w:5 * w].astype(F32)

    ct = cu_tail[...]
    conv3 = (w3_ref[2:3, :] * cu + w3_ref[1:2, :] * _shift_rows(cu, ct, 1, row8)
             + w3_ref[0:1, :] * _shift_rows(cu, ct, 2, row8))
    cu_tail[...] = cu[blk - 8:blk]
    o_ref[:, 0:w] = (gb * conv3).astype(o_ref.dtype)

    xt = xr_tail[...]
    xc = (w4_ref[3:4, :] * xr + w4_ref[2:3, :] * _shift_rows(xr, xt, 1, row8)
          + w4_ref[1:2, :] * _shift_rows(xr, xt, 2, row8)
          + w4_ref[0:1, :] * _shift_rows(xr, xt, 3, row8)) + b4_ref[...]
    xr_tail[...] = xr[blk - 8:blk]

    xcb = xc.astype(BF16)
    r = jax.nn.sigmoid(jnp.dot(xcb, wa_ref[...], preferred_element_type=F32) + ba_ref[...])
    ig = jax.nn.sigmoid(jnp.dot(xcb, wx_ref[...], preferred_element_type=F32) + bx_ref[...])
    log_a = (-LRU_C * r) * _softplus(-lam_ref[...])
    a = jnp.exp(log_a)
    b = jnp.sqrt(1.0 - a * a) * (ig * xc)

    row = lax.broadcasted_iota(jnp.int32, (blk, w), 0)
    d = 1
    while d < blk:
        keep = row >= d
        a_s = jnp.where(keep, pltpu.roll(a, d, 0), 1.0)
        b_s = jnp.where(keep, pltpu.roll(b, d, 0), 0.0)
        b = a * b_s + b
        a = a * a_s
        d *= 2
    h = a * h_prev[...] + b
    h_prev[...] = h[blk - 1:blk]

    c0 = math.sqrt(2.0 / math.pi)
    gelu = 0.5 * xg * (1.0 + jnp.tanh(c0 * (xg + 0.044715 * (xg * xg * xg))))
    o_ref[:, w:2 * w] = (h * gelu).astype(o_ref.dtype)


def _block_diag(wh):
    nh, bi, bo = wh.shape
    eye = jnp.eye(nh, dtype=wh.dtype)
    return (eye[:, None, :, None] * wh[:, :, None, :]).reshape(nh * bi, nh * bo)


def lru_mixer(proj, w3, w4, b4, wa, ba, wx, bx, lam, *, batch, t_len, blk):
    n, pw = proj.shape
    nt = t_len // blk
    w = CONV_W
    row = lambda a: a.reshape(1, w)
    full = lambda shp: pl.BlockSpec(shp, lambda b, t: (0, 0))
    return pl.pallas_call(
        functools.partial(_lru_kernel, blk=blk),
        grid=(batch, nt),
        in_specs=[pl.BlockSpec((blk, pw), lambda b, t: (b * nt + t, 0)),
                  full((8, w)), full((8, w)), full((1, w)),
                  full((w, w)), full((1, w)), full((w, w)), full((1, w)), full((1, w))],
        out_specs=pl.BlockSpec((blk, 2 * w), lambda b, t: (b * nt + t, 0)),
        out_shape=jax.ShapeDtypeStruct((n, 2 * w), BF16),
        scratch_shapes=[pltpu.VMEM((8, w), F32), pltpu.VMEM((8, w), F32), pltpu.VMEM((1, w), F32)],
        compiler_params=_cparams(("parallel", "arbitrary")),
        name="lru_mixer",
    )(proj, jnp.pad(w3, ((0, 8 - w3.shape[0]), (0, 0))), jnp.pad(w4, ((0, 8 - w4.shape[0]), (0, 0))),
      row(b4), _block_diag(wa).astype(BF16), row(ba), _block_diag(wx).astype(BF16), row(bx), row(lam))


def _router_kernel(x_ref, y_ref, wo_ref, g_ref, wh_ref, wl_ref, b_ref, lt_ref, x1_ref, hp_ref, r_ref, c_ref):
    x1 = x_ref[...] + jnp.dot(y_ref[...], wo_ref[...], preferred_element_type=F32)
    x1_ref[...] = x1
    h = _rms(x1, g_ref[...])
    hh = h.astype(BF16)
    hp_ref[...] = hh
    hl = (h - hh.astype(F32)).astype(BF16)
    wh = wh_ref[...]
    logits = (jnp.dot(hh, wh, preferred_element_type=F32) + jnp.dot(hl, wh, preferred_element_type=F32)
              + jnp.dot(hh, wl_ref[...], preferred_element_type=F32)) + b_ref[...]
    lane = lax.broadcasted_iota(jnp.int32, logits.shape, 1).astype(F32)
    neg = jnp.float32(-jnp.inf)
    logits = jnp.where(lane < N_EXPERTS, logits, neg)
    m0 = jnp.max(logits, axis=-1, keepdims=True)
    i0 = jnp.min(jnp.where(logits == m0, lane, float(LANES)), axis=-1, keepdims=True)
    rest = jnp.where(lane == i0, neg, logits)
    m1 = jnp.max(rest, axis=-1, keepdims=True)
    i1 = jnp.min(jnp.where(rest == m1, lane, float(LANES)), axis=-1, keepdims=True)
    e1 = jnp.exp(m1 - m0)
    w0 = 1.0 / (1.0 + e1)
    w1 = e1 * w0
    oh0 = jnp.where(lane == i0, 1.0, 0.0)
    oh1 = jnp.where(lane == i1, 1.0, 0.0)
    oh = oh0 + oh1
    before = jnp.dot(lt_ref[...], oh.astype(BF16), preferred_element_type=F32)
    rank0 = jnp.sum(before * oh0, axis=-1, keepdims=True)
    rank1 = jnp.sum(before * oh1, axis=-1, keepdims=True)
    out = jnp.where(lane == 0, i0, 0.0)
    out = jnp.where(lane == 1, i1, out)
    out = jnp.where(lane == 2, w0, out)
    out = jnp.where(lane == 3, w1, out)
    out = jnp.where(lane == 4, rank0, out)
    out = jnp.where(lane == 5, rank1, out)
    r_ref[...] = out
    c_ref[0] = jnp.sum(oh, axis=0, keepdims=True)


def router(x, y, wo, g, w_router, b_router, *, tm):
    n, d = x.shape
    nt = n // tm
    wp = jnp.pad(w_router, ((0, 0), (0, LANES - N_EXPERTS)))
    wh = wp.astype(BF16)
    wl = (wp - wh.astype(F32)).astype(BF16)
    bp = jnp.pad(b_router, (0, LANES - N_EXPERTS)).reshape(1, LANES)
    t = jnp.arange(tm)
    lower = (t[None, :] < t[:, None]).astype(BF16)
    const = lambda shp: pl.BlockSpec(shp, lambda i: (0, 0))
    return pl.pallas_call(
        _router_kernel,
        grid=(nt,),
        in_specs=[pl.BlockSpec((tm, d), lambda i: (i, 0)), pl.BlockSpec((tm, y.shape[1]), lambda i: (i, 0)),
                  const(wo.shape), const((1, d)), const((d, LANES)),
                  const((d, LANES)), const((1, LANES)), const((tm, tm))],
        out_specs=[pl.BlockSpec((tm, d), lambda i: (i, 0)), pl.BlockSpec((tm, d), lambda i: (i, 0)),
                   pl.BlockSpec((tm, LANES), lambda i: (i, 0)), pl.BlockSpec((1, 1, LANES), lambda i: (i, 0, 0))],
        out_shape=[jax.ShapeDtypeStruct((n, d), F32), jax.ShapeDtypeStruct((n, d), BF16),
                   jax.ShapeDtypeStruct((n, LANES), F32), jax.ShapeDtypeStruct((nt, 1, LANES), F32)],
        compiler_params=_cparams(("parallel",)),
        name="router",
    )(x, y, wo, g.reshape(1, d), wh, wl, bp, lower)


def _slot_plan(r, counts, *, tm, tb):
    n = r.shape[0]
    cnt = counts[:, 0, :N_EXPERTS].astype(jnp.int32)
    base = jnp.cumsum(cnt, axis=0) - cnt
    total = jnp.sum(cnt, axis=0)
    padded = (total + tb - 1) // tb * tb
    pad_end = jnp.cumsum(padded)
    pad_start = pad_end - padded
    off_tok = jnp.repeat(base + pad_start[None, :], tm, axis=0)
    experts = jnp.arange(N_EXPERTS, dtype=jnp.int32)[None, :]

    def dest(col_e, col_rank):
        e = r[:, col_e].astype(jnp.int32)
        return jnp.sum(jnp.where(e[:, None] == experts, off_tok, 0), axis=1) + r[:, col_rank].astype(jnp.int32)

    n_slots = 2 * n + N_EXPERTS * tb
    dest2 = jnp.clip(jnp.stack([dest(0, 4), dest(1, 5)], axis=1).reshape(2 * n), 0, n_slots - 1)
    blk_start = jnp.arange(n_slots // tb, dtype=jnp.int32) * tb
    block_e = jnp.minimum(jnp.sum(blk_start[:, None] >= pad_end[None, :], axis=1), N_EXPERTS - 1)
    n_used = (pad_end[N_EXPERTS - 1] // tb).reshape(1)
    return dest2, block_e.astype(jnp.int32), n_used.astype(jnp.int32), pad_end.astype(jnp.int32), n_slots


def _dispatch_kernel(pe_ref, dest_hbm, h_ref, xb_hbm, dest_a, dest_b, pk, zbuf, sem_d, sem_r, sem_z,
                     *, tm, tb):
    i = pl.program_id(0)
    n = pl.num_programs(0)
    slot = i % 2
    dest_s = (dest_a, dest_b)

    def dest_copy(step, s):
        return pltpu.make_async_copy(dest_hbm.at[pl.ds(step * (2 * tm), 2 * tm)], dest_s[s], sem_d.at[s])

    def wait_rows(s):
        for _ in range(2):
            pltpu.make_async_copy(pk.at[s], pk.at[s], sem_r.at[s]).wait()

    def zero_block(first):
        cp = pltpu.make_async_copy(zbuf, xb_hbm.at[pl.ds(pl.multiple_of(first, tb), tb)], sem_z.at[0])
        cp.start()
        cp.wait()

    @pl.when(i == 0)
    def _():
        zbuf[...] = jnp.zeros_like(zbuf)
        for e in range(N_EXPERTS):
            end = pe_ref[e]
            start = pe_ref[e - 1] if e > 0 else 0
            pl.when(end > start)(functools.partial(zero_block, end - tb))
            tail = pe_ref[N_EXPERTS - 1] + e * tb
            pl.when(tail < xb_hbm.shape[0])(functools.partial(zero_block, tail))
        dest_copy(0, 0).start()

    def step(s):
        dest_copy(i, s).wait()

        @pl.when(i + 1 < n)
        def _():
            dest_copy(i + 1, 1 - s).start()

        @pl.when(i >= 2)
        def _():
            wait_rows(s)

        pk[s] = h_ref[...].astype(F32).reshape(pk.shape[1:])

        def rows(t8, c):
            for u in range(8):
                for k in range(2):
                    a = dest_s[s][t8 * 16 + (2 * u + k)]
                    pltpu.make_async_copy(pk.at[s, t8, u], xb_hbm.at[a], sem_r.at[s]).start(priority=k)
            return c
        lax.fori_loop(0, tm // 8, rows, 0)

        @pl.when(i == n - 1)
        def _():
            wait_rows(s)

            @pl.when(n > 1)
            def _():
                wait_rows(1 - s)

    _for_slot(slot, step)


def dispatch(hp, dest2, pad_end, n_slots, *, tm, tb):
    n, dp = hp.shape
    grid_spec = pltpu.PrefetchScalarGridSpec(
        num_scalar_prefetch=1,
        grid=(n // tm,),
        in_specs=[pl.BlockSpec(memory_space=pl.ANY), pl.BlockSpec((tm, dp), lambda i, pe: (i, 0))],
        out_specs=pl.BlockSpec(memory_space=pl.ANY),
        scratch_shapes=[pltpu.SMEM((2 * tm,), jnp.int32), pltpu.SMEM((2 * tm,), jnp.int32),
                        pltpu.VMEM((2, tm // 8, 8, dp), F32),
                        pltpu.VMEM((tb, dp), F32), pltpu.SemaphoreType.DMA((2,)),
                        pltpu.SemaphoreType.DMA((2,)), pltpu.SemaphoreType.DMA((1,))],
    )
    return pl.pallas_call(
        functools.partial(_dispatch_kernel, tm=tm, tb=tb),
        grid_spec=grid_spec,
        out_shape=jax.ShapeDtypeStruct((n_slots, dp), F32),
        compiler_params=_cparams(("arbitrary",)),
        name="dispatch",
    )(pad_end, dest2, hp)


def _moe_kernel(be_ref, nu_ref, x_ref, wg_hbm, wu_hbm, wd_hbm, o_ref, wg, wu, wd, sem_w, *, fc):
    i = pl.program_id(0)

    @pl.when(i < nu_ref[0])
    def _():
        e = be_ref[i]
        changed = jnp.logical_or(i == 0, e != be_ref[jnp.maximum(i - 1, 0)])

        @pl.when(changed)
        def _():
            copies = [pltpu.make_async_copy(src.at[e], dst, sem_w.at[j])
                      for j, (src, dst) in enumerate(((wg_hbm, wg), (wu_hbm, wu), (wd_hbm, wd)))]
            for c in copies:
                c.start()
            for c in copies:
                c.wait()

        x = x_ref[...].astype(BF16)
        d_exp = wg.shape[1]
        acc = jnp.zeros(o_ref.shape, F32)
        for c0 in range(0, d_exp, fc):
            gate = jnp.dot(x, wg[:, c0:c0 + fc], preferred_element_type=F32)
            up = jnp.dot(x, wu[:, c0:c0 + fc], preferred_element_type=F32)
            act = (gate * jax.nn.sigmoid(gate) * up).astype(BF16)
            acc = acc + jnp.dot(act, wd[c0:c0 + fc, :], preferred_element_type=F32)
        o_ref[...] = acc

    @pl.when(i >= nu_ref[0])
    def _():
        o_ref[...] = jnp.zeros_like(o_ref)


def moe_experts(xb, block_e, n_used, wg, wu, wd, *, tb, fc):
    n_slots, dp = xb.shape
    d = dp
    f = wg.shape[2]
    blk = lambda i, be, nu: (jnp.maximum(jnp.minimum(i, nu[0] - 1), 0), 0)
    grid_spec = pltpu.PrefetchScalarGridSpec(
        num_scalar_prefetch=2,
        grid=(n_slots // tb,),
        in_specs=[pl.BlockSpec((tb, dp), blk),
                  pl.BlockSpec(memory_space=pl.ANY), pl.BlockSpec(memory_space=pl.ANY),
                  pl.BlockSpec(memory_space=pl.ANY)],
        out_specs=pl.BlockSpec((tb, d), lambda i, be, nu: (i, 0)),
        scratch_shapes=[pltpu.VMEM((d, f), BF16), pltpu.VMEM((d, f), BF16), pltpu.VMEM((f, d), BF16),
                        pltpu.SemaphoreType.DMA((3,))],
    )
    return pl.pallas_call(
        functools.partial(_moe_kernel, fc=fc),
        grid_spec=grid_spec,
        out_shape=jax.ShapeDtypeStruct((n_slots, d), F32),
        compiler_params=_cparams(("arbitrary",)),
        name="moe_experts",
    )(block_e, n_used, xb, wg, wu, wd)


def _combine_kernel(dest_hbm, x_ref, r_ref, g_ref, yb_hbm, o_ref, dest_a, dest_b, ybuf, sem_d, sem_g,
                    *, tm):
    i = pl.program_id(0)
    n = pl.num_programs(0)
    slot = i % 2
    dest_s = (dest_a, dest_b)

    def dest_copy(step, s):
        return pltpu.make_async_copy(dest_hbm.at[pl.ds(step * (2 * tm), 2 * tm)], dest_s[s], sem_d.at[s])

    def start_gather(s):
        def rows(t8, c):
            for u in range(8):
                for k in range(2):
                    a = dest_s[s][t8 * 16 + (2 * u + k)]
                    pltpu.make_async_copy(yb_hbm.at[a], ybuf.at[s, k, t8, u], sem_g.at[s]).start(priority=k)
            return c
        lax.fori_loop(0, tm // 8, rows, 0)

    @pl.when(i == 0)
    def _():
        dest_copy(0, 0).start()
        dest_copy(0, 0).wait()
        start_gather(0)

        @pl.when(n > 1)
        def _():
            dest_copy(1, 1).start()

    def step(s):
        @pl.when(i + 1 < n)
        def _():
            dest_copy(i + 1, 1 - s).wait()
            start_gather(1 - s)

        @pl.when(i + 2 < n)
        def _():
            dest_copy(i + 2, s).start()

        for k in range(2):
            pltpu.make_async_copy(ybuf.at[s, k], ybuf.at[s, k], sem_g.at[s]).wait()
        r = r_ref[...]
        y0 = ybuf[s, 0].reshape(x_ref.shape)
        y1 = ybuf[s, 1].reshape(x_ref.shape)
        x = x_ref[...] + r[:, 2:3] * y0 + r[:, 3:4] * y1
        o_ref[...] = _rms(x, g_ref[...])

    _for_slot(slot, step)


def combine_final(x, r, yb, dest2, g, *, tm):
    n, d = x.shape
    return pl.pallas_call(
        functools.partial(_combine_kernel, tm=tm),
        grid=(n // tm,),
        in_specs=[pl.BlockSpec(memory_space=pl.ANY),
                  pl.BlockSpec((tm, d), lambda i: (i, 0)), pl.BlockSpec((tm, LANES), lambda i: (i, 0)),
                  pl.BlockSpec((1, d), lambda i: (0, 0)), pl.BlockSpec(memory_space=pl.ANY)],
        out_specs=pl.BlockSpec((tm, d), lambda i: (i, 0)),
        out_shape=jax.ShapeDtypeStruct((n, d), F32),
        scratch_shapes=[pltpu.SMEM((2 * tm,), jnp.int32), pltpu.SMEM((2 * tm,), jnp.int32),
                        pltpu.VMEM((2, 2, tm // 8, 8, d), F32),
                        pltpu.SemaphoreType.DMA((2,)), pltpu.SemaphoreType.DMA((2,))],
        compiler_params=_cparams(("arbitrary",)),
        name="combine_final",
    )(dest2, x, r, g.reshape(1, d), yb)


def even_layer(x, g_mix, g_ffn, w_in, gn_w, w_out, wg, wu, wd, *, batch, t_len):
    proj = norm_proj(x, g_mix, w_in.astype(BF16), tm=512)
    y_ret = retention(proj, gn_w, batch=batch, t_len=t_len, blk=256)
    y_sb = stick_breaking(proj, batch=batch, t_len=t_len, tq=256, col0=4)
    half = y_ret.shape[1]
    wo = w_out.astype(BF16)
    return ffn(x, g_ffn, [y_ret, y_sb], [wo[:half], wo[half:]],
               wg.astype(BF16), wu.astype(BF16), wd.astype(BF16), tm=512, fc=1408)


def odd_layer_mixer(x, g_mix, w_in, conv_w, lru_conv_w, lru_conv_b, wa, ba, wx, bx, lam, *, batch, t_len):
    proj = norm_proj(x, g_mix, w_in.astype(BF16), tm=512)
    return lru_mixer(proj, conv_w, lru_conv_w, lru_conv_b, wa, ba, wx, bx, lam,
                     batch=batch, t_len=t_len, blk=256)


def moe_and_final(x, y, w_out, g_ffn, g_final, w_router, b_router, wg, wu, wd, *, tm=512, tb=512):
    x, hp, r, counts = router(x, y, w_out.astype(BF16), g_ffn, w_router, b_router, tm=tm)
    dest2, block_e, n_used, pad_end, n_slots = _slot_plan(r, counts, tm=tm, tb=tb)
    xb = dispatch(hp, dest2, pad_end, n_slots, tm=tm, tb=tb)
    yb = moe_experts(xb, block_e, n_used, wg.astype(BF16), wu.astype(BF16), wd.astype(BF16), tb=tb, fc=512)
    return combine_final(x, r, yb, dest2, g_final, tm=256)


def kernel(x, norm_mix, norm_ffn, norm_final, ev_w_in, ev_ret_gn, ev_w_out, ev_ffn_gate, ev_ffn_up,
           ev_ffn_down, od_w_in, od_conv_w, od_lru_conv_w, od_lru_conv_b, od_lru_wa, od_lru_ba,
           od_lru_wx, od_lru_bx, od_lru_lambda, od_w_out, od_router_w, od_router_b, od_exp_gate,
           od_exp_up, od_exp_down):
    batch, t_len, d = x.shape
    xf = x.reshape(batch * t_len, d)
    xf = even_layer(xf, norm_mix[0], norm_ffn[0], ev_w_in[0], ev_ret_gn[0], ev_w_out[0],
                    ev_ffn_gate[0], ev_ffn_up[0], ev_ffn_down[0], batch=batch, t_len=t_len)
    y = odd_layer_mixer(xf, norm_mix[1], od_w_in[0], od_conv_w[0], od_lru_conv_w[0], od_lru_conv_b[0],
                        od_lru_wa[0], od_lru_ba[0], od_lru_wx[0], od_lru_bx[0], od_lru_lambda[0],
                        batch=batch, t_len=t_len)
    out = moe_and_final(xf, y, od_w_out[0], norm_ffn[1], norm_final, od_router_w[0], od_router_b[0],
                        od_exp_gate[0], od_exp_up[0], od_exp_down[0])
    return out.reshape(batch, t_len, d)
```

```python
import functools
import math

import jax
import jax.numpy as jnp
from jax import lax
from jax.experimental import pallas as pl
from jax.experimental.pallas import tpu as pltpu

F32 = jnp.float32
BF16 = jnp.bfloat16

EPS = 1e-6
CHUNK = 64
RET_HEADS = 4
RET_D = 128
SB_PAIRS = 4
SB_DH = 64
ROPE_BASE = 10000.0
CONV_W = 512
LRU_W = 512
LRU_C = 8.0
N_EXPERTS = 8
LANES = 128
VMEM_LIMIT = 56 * 1024 * 1024
SB_DEAD = 104.0


def _cparams(sem):
    return pltpu.CompilerParams(dimension_semantics=sem, vmem_limit_bytes=VMEM_LIMIT)


def _rms(x, g):
    return x * lax.rsqrt(jnp.mean(x * x, axis=-1, keepdims=True) + EPS) * g


def _for_slot(slot, fn):
    for s in range(2):
        pl.when(slot == s)(functools.partial(fn, s))


def _norm_proj_kernel(x_ref, g_ref, w_ref, o_ref):
    h = _rms(x_ref[...], g_ref[...]).astype(BF16)
    o_ref[...] = jnp.dot(h, w_ref[...], preferred_element_type=F32).astype(o_ref.dtype)


def norm_proj(x, g, w, *, tm):
    n, d = x.shape
    m = w.shape[1]
    return pl.pallas_call(
        _norm_proj_kernel,
        grid=(n // tm,),
        in_specs=[
            pl.BlockSpec((tm, d), lambda i: (i, 0)),
            pl.BlockSpec((1, d), lambda i: (0, 0)),
            pl.BlockSpec((d, m), lambda i: (0, 0)),
        ],
        out_specs=pl.BlockSpec((tm, m), lambda i: (i, 0)),
        out_shape=jax.ShapeDtypeStruct((n, m), BF16),
        compiler_params=_cparams(("parallel",)),
        name="norm_proj",
    )(x, g.reshape(1, d), w)


def _retention_kernel(q_ref, k_ref, v_ref, g_ref, cos_ref, sin_ref, dmat_ref, qdec_ref, kdec_ref,
                      sdec_ref, gn_ref, o_ref, s_ref):
    @pl.when(pl.program_id(1) == 0)
    def _():
        s_ref[...] = jnp.zeros_like(s_ref)

    cos = cos_ref[...]
    sin = sin_ref[...]
    half = RET_D // 2
    for h in range(RET_HEADS):
        cols = slice(h * RET_D, (h + 1) * RET_D)
        q = q_ref[:, cols].astype(F32)
        k = k_ref[:, cols].astype(F32)
        q = q * cos + pltpu.roll(q, half, 1) * sin
        k = (k * cos + pltpu.roll(k, half, 1) * sin) * (RET_D ** -0.5)
        qb = q.astype(BF16)
        kb = k.astype(BF16)
        v = v_ref[:, cols]

        s = lax.dot_general(qb, kb, (((1,), (1,)), ((), ())), preferred_element_type=F32)
        s = s * dmat_ref[h]
        o = jnp.dot(s.astype(BF16), v, preferred_element_type=F32)
        state = s_ref[h]
        o = o + jnp.dot(qb, state.astype(BF16), preferred_element_type=F32) * qdec_ref[h]

        kd = (k * kdec_ref[h]).astype(BF16)
        kv = lax.dot_general(kd, v, (((0,), (0,)), ((), ())), preferred_element_type=F32)
        s_ref[h] = state * sdec_ref[h] + kv

        mu = jnp.mean(o, axis=-1, keepdims=True)
        oc = o - mu
        var = jnp.mean(oc * oc, axis=-1, keepdims=True)
        on = oc * lax.rsqrt(var + EPS)
        gate = g_ref[:, cols].astype(F32)
        o_ref[:, cols] = (on * gn_ref[h] * (gate * jax.nn.sigmoid(gate))).astype(o_ref.dtype)


def _retention_tables(t_len, blk):
    h = jnp.arange(RET_HEADS, dtype=F32)
    log_g = jnp.log(1.0 - 2.0 ** (-5.0 - h))
    idx = jnp.arange(blk, dtype=F32)
    diff = idx[:, None] - idx[None, :]
    ci = jnp.arange(blk)[:, None] // CHUNK
    cj = jnp.arange(blk)[None, :] // CHUNK
    expo = jnp.where(ci == cj, jnp.abs(diff), diff)
    dmat = jnp.where(cj <= ci, jnp.exp(log_g[:, None, None] * expo[None]), 0.0)
    qdec = jnp.exp(log_g[:, None] * (idx + 1.0)[None, :])
    kdec = jnp.exp(log_g[:, None] * (blk - 1.0 - idx)[None, :])
    sdec = jnp.exp(log_g * blk)
    bc = lambda a: jnp.broadcast_to(a[:, :, None], (RET_HEADS, blk, RET_D))
    sdec = jnp.broadcast_to(sdec[:, None, None], (RET_HEADS, 1, RET_D))
    inv_freq = 1.0 / (ROPE_BASE ** (jnp.arange(0, RET_D, 2, dtype=F32) / RET_D))
    ang = jnp.arange(t_len, dtype=F32)[:, None] * inv_freq[None, :]
    cos = jnp.cos(ang)
    sin = jnp.sin(ang)
    cos2 = jnp.concatenate([cos, cos], axis=-1)
    sin2 = jnp.concatenate([-sin, sin], axis=-1)
    return cos2, sin2, dmat, bc(qdec), bc(kdec), sdec


def retention(proj, gn_w, *, batch, t_len, blk):
    n = proj.shape[0]
    nt = t_len // blk
    w = RET_HEADS * RET_D
    cos2, sin2, dmat, qdec, kdec, sdec = _retention_tables(t_len, blk)
    gn = gn_w.reshape(RET_HEADS, 1, RET_D)
    col = lambda c: pl.BlockSpec((blk, w), lambda b, t, c=c: (b * nt + t, c))
    tab_t = pl.BlockSpec((blk, RET_D), lambda b, t: (t, 0))
    whole = lambda a: pl.BlockSpec(a.shape, lambda b, t: (0, 0, 0))
    return pl.pallas_call(
        _retention_kernel,
        grid=(batch, nt),
        in_specs=[col(0), col(1), col(2), col(3), tab_t, tab_t,
                  whole(dmat), whole(qdec), whole(kdec), whole(sdec), whole(gn)],
        out_specs=pl.BlockSpec((blk, w), lambda b, t: (b * nt + t, 0)),
        out_shape=jax.ShapeDtypeStruct((n, w), BF16),
        scratch_shapes=[pltpu.VMEM((RET_HEADS, RET_D, RET_D), F32)],
        compiler_params=_cparams(("parallel", "arbitrary")),
        name="retention",
    )(proj, proj, proj, proj, cos2, sin2, dmat, qdec, kdec, sdec, gn)


def _softplus(z):
    return jnp.maximum(z, 0.0) + jnp.log(1.0 + jnp.exp(-jnp.abs(z)))


def _sb_kernel(q_ref, k_ref, v_ref, tri_ref, o_ref, carry_ref, acc_ref, qs_ref, *, tq):
    qi = pl.program_id(1)
    lane = lax.broadcasted_iota(jnp.int32, (1, LANES), 1)
    lo_lanes = lane < SB_DH
    tri2 = tri_ref[...]

    n_heads = 2 * SB_PAIRS
    for p in range(SB_PAIRS):
        q = q_ref[:, p * LANES:(p + 1) * LANES].astype(F32) * (SB_DH ** -0.5)
        qs_ref[(2 * p) * tq:(2 * p + 1) * tq, :] = jnp.where(lo_lanes, q, 0.0).astype(BF16)
        qs_ref[(2 * p + 1) * tq:(2 * p + 2) * tq, :] = jnp.where(lo_lanes, 0.0, q).astype(BF16)

    def tile(kv_start, mask):
        z = jnp.concatenate([
            lax.dot_general(qs_ref[2 * p * tq:(2 * p + 2) * tq, :], k_ref[pl.ds(kv_start, tq), p * LANES:(p + 1) * LANES],
                            (((1,), (1,)), ((), ())), preferred_element_type=F32)
            for p in range(SB_PAIRS)], axis=0)
        sp = _softplus(z)
        if mask is not None:
            mask = jnp.concatenate([mask] * n_heads, axis=0)
            sp = jnp.where(mask, sp, 0.0)
        hi = sp.astype(BF16)
        lo = (sp - hi.astype(F32)).astype(BF16)
        sums = jnp.dot(jnp.concatenate([hi, lo], axis=1), tri2, preferred_element_type=F32)
        carry = carry_ref[...]
        new_carry = carry + jnp.broadcast_to(sums[:, 0:1], carry.shape)
        carry_ref[...] = new_carry
        a = jnp.exp(z - sums - jnp.concatenate([carry] * (tq // LANES), axis=1))
        if mask is not None:
            a = jnp.where(mask, a, 0.0)
        a = a.astype(BF16)
        for p in range(SB_PAIRS):
            cols = slice(p * LANES, (p + 1) * LANES)
            pv = jnp.dot(a[2 * p * tq:(2 * p + 2) * tq, :], v_ref[pl.ds(kv_start, tq), cols],
                         preferred_element_type=F32)
            acc_ref[:, cols] += jnp.where(lo_lanes, pv[0:tq], pv[tq:2 * tq])
        return jnp.min(new_carry)

    carry_ref[...] = jnp.zeros_like(carry_ref)
    acc_ref[...] = jnp.zeros_like(acc_ref)
    row = lax.broadcasted_iota(jnp.int32, (tq, tq), 0)
    colm = lax.broadcasted_iota(jnp.int32, (tq, tq), 1)
    least0 = tile(pl.multiple_of(qi * tq, tq), colm < row)

    def alive(state):
        jj, least = state
        return jnp.logical_and(jj < qi, least < SB_DEAD)

    def body(state):
        jj, _ = state
        return jj + 1, tile(pl.multiple_of((qi - 1 - jj) * tq, tq), None)

    lax.while_loop(alive, body, (jnp.int32(0), least0))
    o_ref[...] = acc_ref[...].astype(o_ref.dtype)


def stick_breaking(proj, *, batch, t_len, tq, col0):
    n = proj.shape[0]
    nq = t_len // tq
    w = SB_PAIRS * LANES
    r = jnp.arange(tq)
    tri = (r[:, None] >= r[None, :]).astype(BF16)
    tri2 = jnp.concatenate([tri, tri], axis=0)
    return pl.pallas_call(
        functools.partial(_sb_kernel, tq=tq),
        grid=(batch, nq),
        in_specs=[pl.BlockSpec((tq, w), lambda b, i: (b * nq + i, col0)),
                  pl.BlockSpec((t_len, w), lambda b, i: (b, col0 + 1)),
                  pl.BlockSpec((t_len, w), lambda b, i: (b, col0 + 2)),
                  pl.BlockSpec(tri2.shape, lambda b, i: (0, 0))],
        out_specs=pl.BlockSpec((tq, w), lambda b, i: (b * nq + i, 0)),
        out_shape=jax.ShapeDtypeStruct((n, w), BF16),
        scratch_shapes=[pltpu.VMEM((2 * SB_PAIRS * tq, LANES), F32), pltpu.VMEM((tq, w), F32),
                        pltpu.VMEM((2 * SB_PAIRS * tq, LANES), BF16)],
        compiler_params=_cparams(("parallel", "arbitrary")),
        name="stick_breaking",
    )(proj, proj, proj, tri2)


def _load_once(pairs, sem):
    @pl.when(pl.program_id(0) == 0)
    def _():
        copies = [pltpu.make_async_copy(src, dst, sem.at[j]) for j, (src, dst) in enumerate(pairs)]
        for c in copies:
            c.start()
        for c in copies:
            c.wait()


def _ffn_kernel(*refs, n_in, fc):
    x_ref, g_ref = refs[0], refs[1]
    y_refs = refs[2:2 + n_in]
    wo_refs = refs[2 + n_in:2 + 2 * n_in]
    wg_hbm, wu_hbm, wd_hbm, o_ref, wg, wu, wd, sem = refs[2 + 2 * n_in:]
    _load_once(((wg_hbm, wg), (wu_hbm, wu), (wd_hbm, wd)), sem)

    x1 = x_ref[...]
    for y_ref, wo_ref in zip(y_refs, wo_refs):
        x1 = x1 + jnp.dot(y_ref[...], wo_ref[...], preferred_element_type=F32)
    h = _rms(x1, g_ref[...]).astype(BF16)
    ff = jnp.zeros_like(x1)
    for c0 in range(0, wg.shape[1], fc):
        gate = jnp.dot(h, wg[:, c0:c0 + fc], preferred_element_type=F32)
        up = jnp.dot(h, wu[:, c0:c0 + fc], preferred_element_type=F32)
        act = (gate * jax.nn.sigmoid(gate) * up).astype(BF16)
        ff = ff + jnp.dot(act, wd[c0:c0 + fc, :], preferred_element_type=F32)
    o_ref[...] = x1 + ff


def ffn(x, g, ys, wos, wg, wu, wd, *, tm, fc):
    n, d = x.shape
    f = wg.shape[1]
    n_in = len(ys)
    hbm = pl.BlockSpec(memory_space=pl.ANY)
    in_specs = [pl.BlockSpec((tm, d), lambda i: (i, 0)), pl.BlockSpec((1, d), lambda i: (0, 0))]
    in_specs += [pl.BlockSpec((tm, y.shape[1]), lambda i: (i, 0)) for y in ys]
    in_specs += [pl.BlockSpec(w.shape, lambda i: (0, 0)) for w in wos]
    in_specs += [hbm, hbm, hbm]
    return pl.pallas_call(
        functools.partial(_ffn_kernel, n_in=n_in, fc=fc),
        grid=(n // tm,),
        in_specs=in_specs,
        out_specs=pl.BlockSpec((tm, d), lambda i: (i, 0)),
        out_shape=jax.ShapeDtypeStruct((n, d), F32),
        scratch_shapes=[pltpu.VMEM((d, f), BF16), pltpu.VMEM((d, f), BF16), pltpu.VMEM((f, d), BF16),
                        pltpu.SemaphoreType.DMA((3,))],
        compiler_params=_cparams(("arbitrary",)),
        name="ffn",
    )(x, g.reshape(1, d), *ys, *wos, wg, wu, wd)


def _shift_rows(cur, tail, d, row8):
    rolled = pltpu.roll(cur, d, 0)
    top = jnp.where(row8 < d, pltpu.roll(tail, d, 0), rolled[0:8])
    return jnp.concatenate([top, rolled[8:]], axis=0)


def _lru_kernel(p_ref, w3_ref, w4_ref, b4_ref, wa_ref, ba_ref, wx_ref, bx_ref, lam_ref, o_ref,
                cu_tail, xr_tail, h_prev, *, blk):
    @pl.when(pl.program_id(1) == 0)
    def _():
        cu_tail[...] = jnp.zeros_like(cu_tail)
        xr_tail[...] = jnp.zeros_like(xr_tail)
        h_prev[...] = jnp.zeros_like(h_prev)

    w = CONV_W
    row8 = lax.broadcasted_iota(jnp.int32, (8, w), 0)
    gb = p_ref[:, 0:w].astype(F32)
    cu = p_ref[:, w:2 * w].astype(F32) * p_ref[:, 2 * w:3 * w].astype(F32)
    xr = p_ref[:, 3 * w:4 * w].astype(F32)
    xg = p_ref[:, 4 * w:5 * w].astype(F32)

    ct = cu_tail[...]
    conv3 = (w3_ref[2:3, :] * cu + w3_ref[1:2, :] * _shift_rows(cu, ct, 1, row8)
             + w3_ref[0:1, :] * _shift_rows(cu, ct, 2, row8))
    cu_tail[...] = cu[blk - 8:blk]
    o_ref[:, 0:w] = (gb * conv3).astype(o_ref.dtype)

    xt = xr_tail[...]
    xc = (w4_ref[3:4, :] * xr + w4_ref[2:3, :] * _shift_rows(xr, xt, 1, row8)
          + w4_ref[1:2, :] * _shift_rows(xr, xt, 2, row8)
          + w4_ref[0:1, :] * _shift_rows(xr, xt, 3, row8)) + b4_ref[...]
    xr_tail[...] = xr[blk - 8:blk]

    xcb = xc.astype(BF16)
    r = jax.nn.sigmoid(jnp.dot(xcb, wa_ref[...], preferred_element_type=F32) + ba_ref[...])
    ig = jax.nn.sigmoid(jnp.dot(xcb, wx_ref[...], preferred_element_type=F32) + bx_ref[...])
    log_a = (-LRU_C * r) * _softplus(-lam_ref[...])
    a = jnp.exp(log_a)
    b = jnp.sqrt(1.0 - a * a) * (ig * xc)

    row = lax.broadcasted_iota(jnp.int32, (blk, w), 0)
    d = 1
    while d < blk:
        keep = row >= d
        a_s = jnp.where(keep, pltpu.roll(a, d, 0), 1.0)
        b_s = jnp.where(keep, pltpu.roll(b, d, 0), 0.0)
        b = a * b_s + b
        a = a * a_s
        d *= 2
    h = a * h_prev[...] + b
    h_prev[...] = h[blk - 1:blk]

    c0 = math.sqrt(2.0 / math.pi)
    gelu = 0.5 * xg * (1.0 + jnp.tanh(c0 * (xg + 0.044715 * (xg * xg * xg))))
    o_ref[:, w:2 * w] = (h * gelu).astype(o_ref.dtype)


def _block_diag(wh):
    nh, bi, bo = wh.shape
    eye = jnp.eye(nh, dtype=wh.dtype)
    return (eye[:, None, :, None] * wh[:, :, None, :]).reshape(nh * bi, nh * bo)


def lru_mixer(proj, w3, w4, b4, wa, ba, wx, bx, lam, *, batch, t_len, blk):
    n, pw = proj.shape
    nt = t_len // blk
    w = CONV_W
    row = lambda a: a.reshape(1, w)
    full = lambda shp: pl.BlockSpec(shp, lambda b, t: (0, 0))
    return pl.pallas_call(
        functools.partial(_lru_kernel, blk=blk),
        grid=(batch, nt),
        in_specs=[pl.BlockSpec((blk, pw), lambda b, t: (b * nt + t, 0)),
                  full((8, w)), full((8, w)), full((1, w)),
                  full((w, w)), full((1, w)), full((w, w)), full((1, w)), full((1, w))],
        out_specs=pl.BlockSpec((blk, 2 * w), lambda b, t: (b * nt + t, 0)),
        out_shape=jax.ShapeDtypeStruct((n, 2 * w), BF16),
        scratch_shapes=[pltpu.VMEM((8, w), F32), pltpu.VMEM((8, w), F32), pltpu.VMEM((1, w), F32)],
        compiler_params=_cparams(("parallel", "arbitrary")),
        name="lru_mixer",
    )(proj, jnp.pad(w3, ((0, 8 - w3.shape[0]), (0, 0))), jnp.pad(w4, ((0, 8 - w4.shape[0]), (0, 0))),
      row(b4), _block_diag(wa).astype(BF16), row(ba), _block_diag(wx).astype(BF16), row(bx), row(lam))


def _router_kernel(x_ref, y_ref, wo_ref, g_ref, wh_ref, wl_ref, b_ref, lt_ref, x1_ref, hp_ref, r_ref, c_ref):
    x1 = x_ref[...] + jnp.dot(y_ref[...], wo_ref[...], preferred_element_type=F32)
    x1_ref[...] = x1
    h = _rms(x1, g_ref[...])
    hh = h.astype(BF16)
    hp_ref[...] = hh
    hl = (h - hh.astype(F32)).astype(BF16)
    wh = wh_ref[...]
    logits = (jnp.dot(hh, wh, preferred_element_type=F32) + jnp.dot(hl, wh, preferred_element_type=F32)
              + jnp.dot(hh, wl_ref[...], preferred_element_type=F32)) + b_ref[...]
    lane = lax.broadcasted_iota(jnp.int32, logits.shape, 1).astype(F32)
    neg = jnp.float32(-jnp.inf)
    logits = jnp.where(lane < N_EXPERTS, logits, neg)
    m0 = jnp.max(logits, axis=-1, keepdims=True)
    i0 = jnp.min(jnp.where(logits == m0, lane, float(LANES)), axis=-1, keepdims=True)
    rest = jnp.where(lane == i0, neg, logits)
    m1 = jnp.max(rest, axis=-1, keepdims=True)
    i1 = jnp.min(jnp.where(rest == m1, lane, float(LANES)), axis=-1, keepdims=True)
    e1 = jnp.exp(m1 - m0)
    w0 = 1.0 / (1.0 + e1)
    w1 = e1 * w0
    oh0 = jnp.where(lane == i0, 1.0, 0.0)
    oh1 = jnp.where(lane == i1, 1.0, 0.0)
    oh = oh0 + oh1
    before = jnp.dot(lt_ref[...], oh.astype(BF16), preferred_element_type=F32)
    rank0 = jnp.sum(before * oh0, axis=-1, keepdims=True)
    rank1 = jnp.sum(before * oh1, axis=-1, keepdims=True)
    out = jnp.where(lane == 0, i0, 0.0)
    out = jnp.where(lane == 1, i1, out)
    out = jnp.where(lane == 2, w0, out)
    out = jnp.where(lane == 3, w1, out)
    out = jnp.where(lane == 4, rank0, out)
    out = jnp.where(lane == 5, rank1, out)
    r_ref[...] = out
    c_ref[0] = jnp.sum(oh, axis=0, keepdims=True)


def router(x, y, wo, g, w_router, b_router, *, tm):
    n, d = x.shape
    nt = n // tm
    wp = jnp.pad(w_router, ((0, 0), (0, LANES - N_EXPERTS)))
    wh = wp.astype(BF16)
    wl = (wp - wh.astype(F32)).astype(BF16)
    bp = jnp.pad(b_router, (0, LANES - N_EXPERTS)).reshape(1, LANES)
    t = jnp.arange(tm)
    lower = (t[None, :] < t[:, None]).astype(BF16)
    const = lambda shp: pl.BlockSpec(shp, lambda i: (0, 0))
    return pl.pallas_call(
        _router_kernel,
        grid=(nt,),
        in_specs=[pl.BlockSpec((tm, d), lambda i: (i, 0)), pl.BlockSpec((tm, y.shape[1]), lambda i: (i, 0)),
                  const(wo.shape), const((1, d)), const((d, LANES)),
                  const((d, LANES)), const((1, LANES)), const((tm, tm))],
        out_specs=[pl.BlockSpec((tm, d), lambda i: (i, 0)), pl.BlockSpec((tm, d), lambda i: (i, 0)),
                   pl.BlockSpec((tm, LANES), lambda i: (i, 0)), pl.BlockSpec((1, 1, LANES), lambda i: (i, 0, 0))],
        out_shape=[jax.ShapeDtypeStruct((n, d), F32), jax.ShapeDtypeStruct((n, d), BF16),
                   jax.ShapeDtypeStruct((n, LANES), F32), jax.ShapeDtypeStruct((nt, 1, LANES), F32)],
        compiler_params=_cparams(("parallel",)),
        name="router",
    )(x, y, wo, g.reshape(1, d), wh, wl, bp, lower)


GRAN = 8


def _run_rows(tm):
    return -(-(2 * tm + N_EXPERTS * GRAN) // LANES) * LANES


def _slot_plan(r, counts, *, tm, tb):
    n = r.shape[0]
    nt = n // tm
    cnt = counts[:, 0, :N_EXPERTS].astype(jnp.int32)
    run = (cnt + GRAN - 1) // GRAN * GRAN
    local = jnp.cumsum(run, axis=1) - run
    total = jnp.sum(run, axis=0)
    padded = (total + tb - 1) // tb * tb
    pad_end = jnp.cumsum(padded)
    first = (pad_end - padded)[None, :] + jnp.cumsum(run, axis=0) - run
    local_tok = jnp.repeat(local, tm, axis=0)
    experts = jnp.arange(N_EXPERTS, dtype=jnp.int32)[None, :]

    def pos(col_e, col_rank):
        e = r[:, col_e].astype(jnp.int32)
        p = jnp.sum(jnp.where(e[:, None] == experts, local_tok, 0), axis=1) + r[:, col_rank].astype(jnp.int32)
        return p.astype(F32)

    pos0, pos1 = pos(0, 4), pos(1, 5)
    lane = jnp.arange(LANES)[None, :]
    r_pos = jnp.where(lane == 4, pos0[:, None], jnp.where(lane == 5, pos1[:, None], r))
    rows = jnp.stack([pos0, pos1, r[:, 2], r[:, 3]], axis=0).reshape(4, nt, tm).transpose(1, 0, 2)
    t_pos = jnp.pad(rows, ((0, 0), (0, 8 - rows.shape[1]), (0, 0)))
    n_slots = -(-(2 * n + nt * N_EXPERTS * GRAN + N_EXPERTS * tb) // tb) * tb
    blk_start = jnp.arange(n_slots // tb, dtype=jnp.int32) * tb
    block_e = jnp.minimum(jnp.sum(blk_start[:, None] >= pad_end[None, :], axis=1), N_EXPERTS - 1)
    n_used = (pad_end[N_EXPERTS - 1] // tb).reshape(1)
    tables = tuple(a.reshape(-1).astype(jnp.int32) for a in (first, run // GRAN, local))
    return (r_pos, t_pos, tables, block_e.astype(jnp.int32), n_used.astype(jnp.int32),
            pad_end.astype(jnp.int32), n_slots)


def _run_copies(tables, step, make_copy, fn):
    first_ref, pieces_ref, local_ref = tables
    for e in range(N_EXPERTS):
        k = step * N_EXPERTS + e
        first = first_ref[k]
        local = local_ref[k]

        def piece(g, c, first=first, local=local):
            fn(make_copy(pl.multiple_of(first + g * GRAN, GRAN), pl.multiple_of(local + g * GRAN, GRAN)))
            return c
        lax.fori_loop(0, pieces_ref[k], piece, 0)


def _dispatch_kernel(first_ref, pieces_ref, local_ref, pe_ref, h_ref, tp_ref, xb_hbm, cbuf, zbuf, sem_r, sem_z,
                     *, tb):
    i = pl.program_id(0)
    n = pl.num_programs(0)
    slot = i % 2
    tables = (first_ref, pieces_ref, local_ref)
    d = h_ref.shape[1]

    def copies(step, s, fn):
        _run_copies(tables, step, lambda row, loc: pltpu.make_async_copy(
            cbuf.at[s, pl.ds(loc, GRAN)], xb_hbm.at[pl.ds(row, GRAN)], sem_r.at[s]), fn)

    def zero_block(first):
        cp = pltpu.make_async_copy(zbuf, xb_hbm.at[pl.ds(pl.multiple_of(first, tb), tb)], sem_z.at[0])
        cp.start()
        cp.wait()

    @pl.when(i == 0)
    def _():
        zbuf[...] = jnp.zeros_like(zbuf)
        for e in range(N_EXPERTS):
            end = pe_ref[e]
            start = pe_ref[e - 1] if e > 0 else 0
            pl.when(end > start)(functools.partial(zero_block, end - tb))
            tail = pe_ref[N_EXPERTS - 1] + e * tb
            pl.when(tail < xb_hbm.shape[0])(functools.partial(zero_block, tail))

    def step(s):
        @pl.when(i >= 2)
        def _():
            copies(i - 2, s, lambda c: c.wait())

        cr, tm = cbuf.shape[1], h_ref.shape[0]
        pos0, pos1 = tp_ref[0:1, :], tp_ref[1:2, :]
        w0, w1 = tp_ref[2:3, :], tp_ref[3:4, :]
        j = lax.broadcasted_iota(jnp.int32, (cr, tm), 0).astype(F32)
        m0 = j == pos0
        m1 = j == pos1
        pick = jnp.where(jnp.logical_or(m0, m1), 1.0, 0.0).astype(BF16)
        cbuf[s, :, 0:d] = jnp.dot(pick, h_ref[...], preferred_element_type=F32)
        wsel = jnp.sum(jnp.where(m0, w0, 0.0) + jnp.where(m1, w1, 0.0), axis=1, keepdims=True)
        cbuf[s, :, d:d + LANES] = jnp.broadcast_to(wsel, (cr, LANES))
        copies(i, s, lambda c: c.start())

        @pl.when(i == n - 1)
        def _():
            copies(i, s, lambda c: c.wait())

            @pl.when(n > 1)
            def _():
                copies(i - 1, 1 - s, lambda c: c.wait())

    _for_slot(slot, step)


def dispatch(hp, t_pos, tables, pad_end, n_slots, *, tm, tb):
    n, d = hp.shape
    cr = _run_rows(tm)
    grid_spec = pltpu.PrefetchScalarGridSpec(
        num_scalar_prefetch=4,
        grid=(n // tm,),
        in_specs=[pl.BlockSpec((tm, d), lambda i, *_: (i, 0)),
                  pl.BlockSpec((None, 8, tm), lambda i, *_: (i, 0, 0))],
        out_specs=pl.BlockSpec(memory_space=pl.ANY),
        scratch_shapes=[pltpu.VMEM((2, cr, d + LANES), F32), pltpu.VMEM((tb, d + LANES), F32),
                        pltpu.SemaphoreType.DMA((2,)), pltpu.SemaphoreType.DMA((1,))],
    )
    return pl.pallas_call(
        functools.partial(_dispatch_kernel, tb=tb),
        grid_spec=grid_spec,
        out_shape=jax.ShapeDtypeStruct((n_slots, d + LANES), F32),
        compiler_params=_cparams(("arbitrary",)),
        name="dispatch",
    )(*tables, pad_end, hp, t_pos)


def _moe_kernel(be_ref, nu_ref, x_ref, wg_hbm, wu_hbm, wd_hbm, o_ref, wg, wu, wd, sem_w, *, fc):
    i = pl.program_id(0)

    @pl.when(i < nu_ref[0])
    def _():
        e = be_ref[i]
        changed = jnp.logical_or(i == 0, e != be_ref[jnp.maximum(i - 1, 0)])

        @pl.when(changed)
        def _():
            copies = [pltpu.make_async_copy(src.at[e], dst, sem_w.at[j])
                      for j, (src, dst) in enumerate(((wg_hbm, wg), (wu_hbm, wu), (wd_hbm, wd)))]
            for c in copies:
                c.start()
            for c in copies:
                c.wait()

        d = o_ref.shape[1]
        x = x_ref[:, 0:d].astype(BF16)
        d_exp = wg.shape[1]
        acc = jnp.zeros(o_ref.shape, F32)
        for c0 in range(0, d_exp, fc):
            gate = jnp.dot(x, wg[:, c0:c0 + fc], preferred_element_type=F32)
            up = jnp.dot(x, wu[:, c0:c0 + fc], preferred_element_type=F32)
            act = (gate * jax.nn.sigmoid(gate) * up).astype(BF16)
            acc = acc + jnp.dot(act, wd[c0:c0 + fc, :], preferred_element_type=F32)
        o_ref[...] = acc * x_ref[:, d:d + 1]

    @pl.when(i >= nu_ref[0])
    def _():
        o_ref[...] = jnp.zeros_like(o_ref)


def moe_experts(xb, block_e, n_used, wg, wu, wd, *, tb, fc):
    n_slots, dp = xb.shape
    d = dp - LANES
    f = wg.shape[2]
    blk = lambda i, be, nu: (jnp.maximum(jnp.minimum(i, nu[0] - 1), 0), 0)
    grid_spec = pltpu.PrefetchScalarGridSpec(
        num_scalar_prefetch=2,
        grid=(n_slots // tb,),
        in_specs=[pl.BlockSpec((tb, dp), blk),
                  pl.BlockSpec(memory_space=pl.ANY), pl.BlockSpec(memory_space=pl.ANY),
                  pl.BlockSpec(memory_space=pl.ANY)],
        out_specs=pl.BlockSpec((tb, d), lambda i, be, nu: (i, 0)),
        scratch_shapes=[pltpu.VMEM((d, f), BF16), pltpu.VMEM((d, f), BF16), pltpu.VMEM((f, d), BF16),
                        pltpu.SemaphoreType.DMA((3,))],
    )
    return pl.pallas_call(
        functools.partial(_moe_kernel, fc=fc),
        grid_spec=grid_spec,
        out_shape=jax.ShapeDtypeStruct((n_slots, d), F32),
        compiler_params=_cparams(("arbitrary",)),
        name="moe_experts",
    )(block_e, n_used, xb, wg, wu, wd)


def _combine_kernel(first_ref, pieces_ref, local_ref, x_ref, r_ref, g_ref, yb_hbm, o_ref, ybuf, sem_g):
    i = pl.program_id(0)
    n = pl.num_programs(0)
    slot = i % 2
    tables = (first_ref, pieces_ref, local_ref)

    def copies(step, s, fn):
        _run_copies(tables, step, lambda row, loc: pltpu.make_async_copy(
            yb_hbm.at[pl.ds(row, GRAN)], ybuf.at[s, pl.ds(loc, GRAN)], sem_g.at[s]), fn)

    @pl.when(i == 0)
    def _():
        ybuf[...] = jnp.zeros_like(ybuf)
        copies(0, 0, lambda c: c.start())

    def step(s):
        @pl.when(i + 1 < n)
        def _():
            copies(i + 1, 1 - s, lambda c: c.start())

        copies(i, s, lambda c: c.wait())
        tm, cr = x_ref.shape[0], ybuf.shape[1]
        r = r_ref[...]
        j = lax.broadcasted_iota(jnp.int32, (tm, cr), 1).astype(F32)
        pick = jnp.where(jnp.logical_or(j == r[:, 4:5], j == r[:, 5:6]), 1.0, 0.0).astype(BF16)
        y = jnp.dot(pick, ybuf[s].astype(BF16), preferred_element_type=F32)
        o_ref[...] = _rms(x_ref[...] + y, g_ref[...])

    _for_slot(slot, step)


def combine_final(x, r_pos, yb, tables, g, *, tm):
    n, d = x.shape
    grid_spec = pltpu.PrefetchScalarGridSpec(
        num_scalar_prefetch=3,
        grid=(n // tm,),
        in_specs=[pl.BlockSpec((tm, d), lambda i, *_: (i, 0)), pl.BlockSpec((tm, LANES), lambda i, *_: (i, 0)),
                  pl.BlockSpec((1, d), lambda i, *_: (0, 0)), pl.BlockSpec(memory_space=pl.ANY)],
        out_specs=pl.BlockSpec((tm, d), lambda i, *_: (i, 0)),
        scratch_shapes=[pltpu.VMEM((2, _run_rows(tm), d), F32), pltpu.SemaphoreType.DMA((2,))],
    )
    return pl.pallas_call(
        _combine_kernel,
        grid_spec=grid_spec,
        out_shape=jax.ShapeDtypeStruct((n, d), F32),
        compiler_params=_cparams(("arbitrary",)),
        name="combine_final",
    )(*tables, x, r_pos, g.reshape(1, d), yb)


def even_layer(x, g_mix, g_ffn, w_in, gn_w, w_out, wg, wu, wd, *, batch, t_len):
    proj = norm_proj(x, g_mix, w_in.astype(BF16), tm=512)
    y_ret = retention(proj, gn_w, batch=batch, t_len=t_len, blk=256)
    y_sb = stick_breaking(proj, batch=batch, t_len=t_len, tq=256, col0=4)
    half = y_ret.shape[1]
    wo = w_out.astype(BF16)
    return ffn(x, g_ffn, [y_ret, y_sb], [wo[:half], wo[half:]],
               wg.astype(BF16), wu.astype(BF16), wd.astype(BF16), tm=512, fc=1408)


def odd_layer_mixer(x, g_mix, w_in, conv_w, lru_conv_w, lru_conv_b, wa, ba, wx, bx, lam, *, batch, t_len):
    proj = norm_proj(x, g_mix, w_in.astype(BF16), tm=512)
    return lru_mixer(proj, conv_w, lru_conv_w, lru_conv_b, wa, ba, wx, bx, lam,
                     batch=batch, t_len=t_len, blk=256)


def moe_and_final(x, y, w_out, g_ffn, g_final, w_router, b_router, wg, wu, wd, *, tm=512, tb=512):
    x, hp, r, counts = router(x, y, w_out.astype(BF16), g_ffn, w_router, b_router, tm=tm)
    r_pos, t_pos, tables, block_e, n_used, pad_end, n_slots = _slot_plan(r, counts, tm=tm, tb=tb)
    xb = dispatch(hp, t_pos, tables, pad_end, n_slots, tm=tm, tb=tb)
    yb = moe_experts(xb, block_e, n_used, wg.astype(BF16), wu.astype(BF16), wd.astype(BF16), tb=tb, fc=512)
    return combine_final(x, r_pos, yb, tables, g_final, tm=tm)


def kernel(x, norm_mix, norm_ffn, norm_final, ev_w_in, ev_ret_gn, ev_w_out, ev_ffn_gate, ev_ffn_up,
           ev_ffn_down, od_w_in, od_conv_w, od_lru_conv_w, od_lru_conv_b, od_lru_wa, od_lru_ba,
           od_lru_wx, od_lru_bx, od_lru_lambda, od_w_out, od_router_w, od_router_b, od_exp_gate,
           od_exp_up, od_exp_down):
    batch, t_len, d = x.shape
    xf = x.reshape(batch * t_len, d)
    xf = even_layer(xf, norm_mix[0], norm_ffn[0], ev_w_in[0], ev_ret_gn[0], ev_w_out[0],
                    ev_ffn_gate[0], ev_ffn_up[0], ev_ffn_down[0], batch=batch, t_len=t_len)
    y = odd_layer_mixer(xf, norm_mix[1], od_w_in[0], od_conv_w[0], od_lru_conv_w[0], od_lru_conv_b[0],
                        od_lru_wa[0], od_lru_ba[0], od_lru_wx[0], od_lru_bx[0], od_lru_lambda[0],
                        batch=batch, t_len=t_len)
    out = moe_and_final(xf, y, od_w_out[0], norm_ffn[1], norm_final, od_router_w[0], od_router_b[0],
                        od_exp_gate[0], od_exp_up[0], od_exp_down[0])
    return out.reshape(batch, t_len, d)
```

```python
import functools
import math

import jax
import jax.numpy as jnp
from jax import lax
from jax.experimental import pallas as pl
from jax.experimental.pallas import tpu as pltpu

F32 = jnp.float32
BF16 = jnp.bfloat16

EPS = 1e-6
CHUNK = 64
RET_HEADS = 4
RET_D = 128
SB_PAIRS = 4
SB_DH = 64
ROPE_BASE = 10000.0
CONV_W = 512
LRU_W = 512
LRU_C = 8.0
N_EXPERTS = 8
LANES = 128
VMEM_LIMIT = 56 * 1024 * 1024
SB_DEAD = 104.0


def _cparams(sem):
    return pltpu.CompilerParams(dimension_semantics=sem, vmem_limit_bytes=VMEM_LIMIT)


def _rms(x, g):
    return x * lax.rsqrt(jnp.mean(x * x, axis=-1, keepdims=True) + EPS) * g


def _for_slot(slot, fn):
    for s in range(2):
        pl.when(slot == s)(functools.partial(fn, s))


def _norm_proj_kernel(x_ref, g_ref, w_ref, o_ref):
    h = _rms(x_ref[...], g_ref[...]).astype(BF16)
    o_ref[...] = jnp.dot(h, w_ref[...], preferred_element_type=F32).astype(o_ref.dtype)


def norm_proj(x, g, w, *, tm):
    n, d = x.shape
    m = w.shape[1]
    return pl.pallas_call(
        _norm_proj_kernel,
        grid=(n // tm,),
        in_specs=[
            pl.BlockSpec((tm, d), lambda i: (i, 0)),
            pl.BlockSpec((1, d), lambda i: (0, 0)),
            pl.BlockSpec((d, m), lambda i: (0, 0)),
        ],
        out_specs=pl.BlockSpec((tm, m), lambda i: (i, 0)),
        out_shape=jax.ShapeDtypeStruct((n, m), BF16),
        compiler_params=_cparams(("parallel",)),
        name="norm_proj",
    )(x, g.reshape(1, d), w)


def _retention_kernel(q_ref, k_ref, v_ref, g_ref, cos_ref, sin_ref, dmat_ref, qdec_ref, kdec_ref,
                      sdec_ref, gn_ref, o_ref, s_ref):
    @pl.when(pl.program_id(1) == 0)
    def _():
        s_ref[...] = jnp.zeros_like(s_ref)

    cos = cos_ref[...]
    sin = sin_ref[...]
    half = RET_D // 2
    for h in range(RET_HEADS):
        cols = slice(h * RET_D, (h + 1) * RET_D)
        q = q_ref[:, cols].astype(F32)
        k = k_ref[:, cols].astype(F32)
        q = q * cos + pltpu.roll(q, half, 1) * sin
        k = (k * cos + pltpu.roll(k, half, 1) * sin) * (RET_D ** -0.5)
        qb = q.astype(BF16)
        kb = k.astype(BF16)
        v = v_ref[:, cols]

        s = lax.dot_general(qb, kb, (((1,), (1,)), ((), ())), preferred_element_type=F32)
        s = s * dmat_ref[h]
        o = jnp.dot(s.astype(BF16), v, preferred_element_type=F32)
        state = s_ref[h]
        o = o + jnp.dot(qb, state.astype(BF16), preferred_element_type=F32) * qdec_ref[h]

        kd = (k * kdec_ref[h]).astype(BF16)
        kv = lax.dot_general(kd, v, (((0,), (0,)), ((), ())), preferred_element_type=F32)
        s_ref[h] = state * sdec_ref[h] + kv

        mu = jnp.mean(o, axis=-1, keepdims=True)
        oc = o - mu
        var = jnp.mean(oc * oc, axis=-1, keepdims=True)
        on = oc * lax.rsqrt(var + EPS)
        gate = g_ref[:, cols].astype(F32)
        o_ref[:, cols] = (on * gn_ref[h] * (gate * jax.nn.sigmoid(gate))).astype(o_ref.dtype)


def _retention_tables(t_len, blk):
    h = jnp.arange(RET_HEADS, dtype=F32)
    log_g = jnp.log(1.0 - 2.0 ** (-5.0 - h))
    idx = jnp.arange(blk, dtype=F32)
    diff = idx[:, None] - idx[None, :]
    ci = jnp.arange(blk)[:, None] // CHUNK
    cj = jnp.arange(blk)[None, :] // CHUNK
    expo = jnp.where(ci == cj, jnp.abs(diff), diff)
    dmat = jnp.where(cj <= ci, jnp.exp(log_g[:, None, None] * expo[None]), 0.0)
    qdec = jnp.exp(log_g[:, None] * (idx + 1.0)[None, :])
    kdec = jnp.exp(log_g[:, None] * (blk - 1.0 - idx)[None, :])
    sdec = jnp.exp(log_g * blk)
    bc = lambda a: jnp.broadcast_to(a[:, :, None], (RET_HEADS, blk, RET_D))
    sdec = jnp.broadcast_to(sdec[:, None, None], (RET_HEADS, 1, RET_D))
    inv_freq = 1.0 / (ROPE_BASE ** (jnp.arange(0, RET_D, 2, dtype=F32) / RET_D))
    ang = jnp.arange(t_len, dtype=F32)[:, None] * inv_freq[None, :]
    cos = jnp.cos(ang)
    sin = jnp.sin(ang)
    cos2 = jnp.concatenate([cos, cos], axis=-1)
    sin2 = jnp.concatenate([-sin, sin], axis=-1)
    return cos2, sin2, dmat, bc(qdec), bc(kdec), sdec


def retention(proj, gn_w, *, batch, t_len, blk):
    n = proj.shape[0]
    nt = t_len // blk
    w = RET_HEADS * RET_D
    cos2, sin2, dmat, qdec, kdec, sdec = _retention_tables(t_len, blk)
    gn = gn_w.reshape(RET_HEADS, 1, RET_D)
    col = lambda c: pl.BlockSpec((blk, w), lambda b, t, c=c: (b * nt + t, c))
    tab_t = pl.BlockSpec((blk, RET_D), lambda b, t: (t, 0))
    whole = lambda a: pl.BlockSpec(a.shape, lambda b, t: (0, 0, 0))
    return pl.pallas_call(
        _retention_kernel,
        grid=(batch, nt),
        in_specs=[col(0), col(1), col(2), col(3), tab_t, tab_t,
                  whole(dmat), whole(qdec), whole(kdec), whole(sdec), whole(gn)],
        out_specs=pl.BlockSpec((blk, w), lambda b, t: (b * nt + t, 0)),
        out_shape=jax.ShapeDtypeStruct((n, w), BF16),
        scratch_shapes=[pltpu.VMEM((RET_HEADS, RET_D, RET_D), F32)],
        compiler_params=_cparams(("parallel", "arbitrary")),
        name="retention",
    )(proj, proj, proj, proj, cos2, sin2, dmat, qdec, kdec, sdec, gn)


def _softplus(z):
    return jnp.maximum(z, 0.0) + jnp.log(1.0 + jnp.exp(-jnp.abs(z)))


def _sb_kernel(q_ref, k_ref, v_ref, tri_ref, o_ref, carry_ref, acc_ref, qs_ref, *, tq):
    qi = pl.program_id(1)
    lane = lax.broadcasted_iota(jnp.int32, (1, LANES), 1)
    lo_lanes = lane < SB_DH
    tri2 = tri_ref[...]

    n_heads = 2 * SB_PAIRS
    for p in range(SB_PAIRS):
        q = q_ref[:, p * LANES:(p + 1) * LANES].astype(F32) * (SB_DH ** -0.5)
        qs_ref[(2 * p) * tq:(2 * p + 1) * tq, :] = jnp.where(lo_lanes, q, 0.0).astype(BF16)
        qs_ref[(2 * p + 1) * tq:(2 * p + 2) * tq, :] = jnp.where(lo_lanes, 0.0, q).astype(BF16)

    def tile(kv_start, mask):
        z = jnp.concatenate([
            lax.dot_general(qs_ref[2 * p * tq:(2 * p + 2) * tq, :], k_ref[pl.ds(kv_start, tq), p * LANES:(p + 1) * LANES],
                            (((1,), (1,)), ((), ())), preferred_element_type=F32)
            for p in range(SB_PAIRS)], axis=0)
        sp = _softplus(z)
        if mask is not None:
            mask = jnp.concatenate([mask] * n_heads, axis=0)
            sp = jnp.where(mask, sp, 0.0)
        hi = sp.astype(BF16)
        lo = (sp - hi.astype(F32)).astype(BF16)
        sums = jnp.dot(jnp.concatenate([hi, lo], axis=1), tri2, preferred_element_type=F32)
        carry = carry_ref[...]
        new_carry = carry + jnp.broadcast_to(sums[:, 0:1], carry.shape)
        carry_ref[...] = new_carry
        a = jnp.exp(z - sums - jnp.concatenate([carry] * (tq // LANES), axis=1))
        if mask is not None:
            a = jnp.where(mask, a, 0.0)
        a = a.astype(BF16)
        for p in range(SB_PAIRS):
            cols = slice(p * LANES, (p + 1) * LANES)
            pv = jnp.dot(a[2 * p * tq:(2 * p + 2) * tq, :], v_ref[pl.ds(kv_start, tq), cols],
                         preferred_element_type=F32)
            acc_ref[:, cols] += jnp.where(lo_lanes, pv[0:tq], pv[tq:2 * tq])
        return jnp.min(new_carry)

    carry_ref[...] = jnp.zeros_like(carry_ref)
    acc_ref[...] = jnp.zeros_like(acc_ref)
    row = lax.broadcasted_iota(jnp.int32, (tq, tq), 0)
    colm = lax.broadcasted_iota(jnp.int32, (tq, tq), 1)
    least0 = tile(pl.multiple_of(qi * tq, tq), colm < row)

    def alive(state):
        jj, least = state
        return jnp.logical_and(jj < qi, least < SB_DEAD)

    def body(state):
        jj, _ = state
        return jj + 1, tile(pl.multiple_of((qi - 1 - jj) * tq, tq), None)

    lax.while_loop(alive, body, (jnp.int32(0), least0))
    o_ref[...] = acc_ref[...].astype(o_ref.dtype)


def stick_breaking(proj, *, batch, t_len, tq, col0):
    n = proj.shape[0]
    nq = t_len // tq
    w = SB_PAIRS * LANES
    r = jnp.arange(tq)
    tri = (r[:, None] >= r[None, :]).astype(BF16)
    tri2 = jnp.concatenate([tri, tri], axis=0)
    return pl.pallas_call(
        functools.partial(_sb_kernel, tq=tq),
        grid=(batch, nq),
        in_specs=[pl.BlockSpec((tq, w), lambda b, i: (b * nq + i, col0)),
                  pl.BlockSpec((t_len, w), lambda b, i: (b, col0 + 1)),
                  pl.BlockSpec((t_len, w), lambda b, i: (b, col0 + 2)),
                  pl.BlockSpec(tri2.shape, lambda b, i: (0, 0))],
        out_specs=pl.BlockSpec((tq, w), lambda b, i: (b * nq + i, 0)),
        out_shape=jax.ShapeDtypeStruct((n, w), BF16),
        scratch_shapes=[pltpu.VMEM((2 * SB_PAIRS * tq, LANES), F32), pltpu.VMEM((tq, w), F32),
                        pltpu.VMEM((2 * SB_PAIRS * tq, LANES), BF16)],
        compiler_params=_cparams(("parallel", "arbitrary")),
        name="stick_breaking",
    )(proj, proj, proj, tri2)


def _load_once(pairs, sem):
    @pl.when(pl.program_id(0) == 0)
    def _():
        copies = [pltpu.make_async_copy(src, dst, sem.at[j]) for j, (src, dst) in enumerate(pairs)]
        for c in copies:
            c.start()
        for c in copies:
            c.wait()


def _ffn_kernel(*refs, n_in, fc):
    x_ref, g_ref = refs[0], refs[1]
    y_refs = refs[2:2 + n_in]
    wo_refs = refs[2 + n_in:2 + 2 * n_in]
    wg_hbm, wu_hbm, wd_hbm, o_ref, wg, wu, wd, sem = refs[2 + 2 * n_in:]
    _load_once(((wg_hbm, wg), (wu_hbm, wu), (wd_hbm, wd)), sem)

    x1 = x_ref[...]
    for y_ref, wo_ref in zip(y_refs, wo_refs):
        x1 = x1 + jnp.dot(y_ref[...], wo_ref[...], preferred_element_type=F32)
    h = _rms(x1, g_ref[...]).astype(BF16)
    ff = jnp.zeros_like(x1)
    for c0 in range(0, wg.shape[1], fc):
        gate = jnp.dot(h, wg[:, c0:c0 + fc], preferred_element_type=F32)
        up = jnp.dot(h, wu[:, c0:c0 + fc], preferred_element_type=F32)
        act = (gate * jax.nn.sigmoid(gate) * up).astype(BF16)
        ff = ff + jnp.dot(act, wd[c0:c0 + fc, :], preferred_element_type=F32)
    o_ref[...] = x1 + ff


def ffn(x, g, ys, wos, wg, wu, wd, *, tm, fc):
    n, d = x.shape
    f = wg.shape[1]
    n_in = len(ys)
    hbm = pl.BlockSpec(memory_space=pl.ANY)
    in_specs = [pl.BlockSpec((tm, d), lambda i: (i, 0)), pl.BlockSpec((1, d), lambda i: (0, 0))]
    in_specs += [pl.BlockSpec((tm, y.shape[1]), lambda i: (i, 0)) for y in ys]
    in_specs += [pl.BlockSpec(w.shape, lambda i: (0, 0)) for w in wos]
    in_specs += [hbm, hbm, hbm]
    return pl.pallas_call(
        functools.partial(_ffn_kernel, n_in=n_in, fc=fc),
        grid=(n // tm,),
        in_specs=in_specs,
        out_specs=pl.BlockSpec((tm, d), lambda i: (i, 0)),
        out_shape=jax.ShapeDtypeStruct((n, d), F32),
        scratch_shapes=[pltpu.VMEM((d, f), BF16), pltpu.VMEM((d, f), BF16), pltpu.VMEM((f, d), BF16),
                        pltpu.SemaphoreType.DMA((3,))],
        compiler_params=_cparams(("arbitrary",)),
        name="ffn",
    )(x, g.reshape(1, d), *ys, *wos, wg, wu, wd)


def _shift_rows(cur, tail, d, row8):
    rolled = pltpu.roll(cur, d, 0)
    top = jnp.where(row8 < d, pltpu.roll(tail, d, 0), rolled[0:8])
    return jnp.concatenate([top, rolled[8:]], axis=0)


def _lru_kernel(p_ref, w3_ref, w4_ref, b4_ref, wa_ref, ba_ref, wx_ref, bx_ref, lam_ref, o_ref,
                cu_tail, xr_tail, h_prev, *, blk):
    @pl.when(pl.program_id(1) == 0)
    def _():
        cu_tail[...] = jnp.zeros_like(cu_tail)
        xr_tail[...] = jnp.zeros_like(xr_tail)
        h_prev[...] = jnp.zeros_like(h_prev)

    w = CONV_W
    row8 = lax.broadcasted_iota(jnp.int32, (8, w), 0)
    gb = p_ref[:, 0:w].astype(F32)
    cu = p_ref[:, w:2 * w].astype(F32) * p_ref[:, 2 * w:3 * w].astype(F32)
    xr = p_ref[:, 3 * w:4 * w].astype(F32)
    xg = p_ref[:, 4 * w:5 * w].astype(F32)

    ct = cu_tail[...]
    conv3 = (w3_ref[2:3, :] * cu + w3_ref[1:2, :] * _shift_rows(cu, ct, 1, row8)
             + w3_ref[0:1, :] * _shift_rows(cu, ct, 2, row8))
    cu_tail[...] = cu[blk - 8:blk]
    o_ref[:, 0:w] = (gb * conv3).astype(o_ref.dtype)

    xt = xr_tail[...]
    xc = (w4_ref[3:4, :] * xr + w4_ref[2:3, :] * _shift_rows(xr, xt, 1, row8)
          + w4_ref[1:2, :] * _shift_rows(xr, xt, 2, row8)
          + w4_ref[0:1, :] * _shift_rows(xr, xt, 3, row8)) + b4_ref[...]
    xr_tail[...] = xr[blk - 8:blk]

    xcb = xc.astype(BF16)
    r = jax.nn.sigmoid(jnp.dot(xcb, wa_ref[...], preferred_element_type=F32) + ba_ref[...])
    ig = jax.nn.sigmoid(jnp.dot(xcb, wx_ref[...], preferred_element_type=F32) + bx_ref[...])
    log_a = (-LRU_C * r) * _softplus(-lam_ref[...])
    a = jnp.exp(log_a)
    b = jnp.sqrt(1.0 - a * a) * (ig * xc)

    row = lax.broadcasted_iota(jnp.int32, (blk, w), 0)
    d = 1
    while d < blk:
        keep = row >= d
        a_s = jnp.where(keep, pltpu.roll(a, d, 0), 1.0)
        b_s = jnp.where(keep, pltpu.roll(b, d, 0), 0.0)
        b = a * b_s + b
        a = a * a_s
        d *= 2
    h = a * h_prev[...] + b
    h_prev[...] = h[blk - 1:blk]

    c0 = math.sqrt(2.0 / math.pi)
    gelu = 0.5 * xg * (1.0 + jnp.tanh(c0 * (xg + 0.044715 * (xg * xg * xg))))
    o_ref[:, w:2 * w] = (h * gelu).astype(o_ref.dtype)


def _block_diag(wh):
    nh, bi, bo = wh.shape
    eye = jnp.eye(nh, dtype=wh.dtype)
    return (eye[:, None, :, None] * wh[:, :, None, :]).reshape(nh * bi, nh * bo)


def lru_mixer(proj, w3, w4, b4, wa, ba, wx, bx, lam, *, batch, t_len, blk):
    n, pw = proj.shape
    nt = t_len // blk
    w = CONV_W
    row = lambda a: a.reshape(1, w)
    full = lambda shp: pl.BlockSpec(shp, lambda b, t: (0, 0))
    return pl.pallas_call(
        functools.partial(_lru_kernel, blk=blk),
        grid=(batch, nt),
        in_specs=[pl.BlockSpec((blk, pw), lambda b, t: (b * nt + t, 0)),
                  full((8, w)), full((8, w)), full((1, w)),
                  full((w, w)), full((1, w)), full((w, w)), full((1, w)), full((1, w))],
        out_specs=pl.BlockSpec((blk, 2 * w), lambda b, t: (b * nt + t, 0)),
        out_shape=jax.ShapeDtypeStruct((n, 2 * w), BF16),
        scratch_shapes=[pltpu.VMEM((8, w), F32), pltpu.VMEM((8, w), F32), pltpu.VMEM((1, w), F32)],
        compiler_params=_cparams(("parallel", "arbitrary")),
        name="lru_mixer",
    )(proj, jnp.pad(w3, ((0, 8 - w3.shape[0]), (0, 0))), jnp.pad(w4, ((0, 8 - w4.shape[0]), (0, 0))),
      row(b4), _block_diag(wa).astype(BF16), row(ba), _block_diag(wx).astype(BF16), row(bx), row(lam))


def _router_kernel(x_ref, y_ref, wo_ref, g_ref, wh_ref, wl_ref, b_ref, lt_ref, x1_ref, hp_ref, r_ref, c_ref):
    x1 = x_ref[...] + jnp.dot(y_ref[...], wo_ref[...], preferred_element_type=F32)
    x1_ref[...] = x1
    h = _rms(x1, g_ref[...])
    hh = h.astype(BF16)
    hp_ref[...] = hh
    hl = (h - hh.astype(F32)).astype(BF16)
    wh = wh_ref[...]
    logits = (jnp.dot(hh, wh, preferred_element_type=F32) + jnp.dot(hl, wh, preferred_element_type=F32)
              + jnp.dot(hh, wl_ref[...], preferred_element_type=F32)) + b_ref[...]
    lane = lax.broadcasted_iota(jnp.int32, logits.shape, 1).astype(F32)
    neg = jnp.float32(-jnp.inf)
    logits = jnp.where(lane < N_EXPERTS, logits, neg)
    m0 = jnp.max(logits, axis=-1, keepdims=True)
    i0 = jnp.min(jnp.where(logits == m0, lane, float(LANES)), axis=-1, keepdims=True)
    rest = jnp.where(lane == i0, neg, logits)
    m1 = jnp.max(rest, axis=-1, keepdims=True)
    i1 = jnp.min(jnp.where(rest == m1, lane, float(LANES)), axis=-1, keepdims=True)
    e1 = jnp.exp(m1 - m0)
    w0 = 1.0 / (1.0 + e1)
    w1 = e1 * w0
    oh0 = jnp.where(lane == i0, 1.0, 0.0)
    oh1 = jnp.where(lane == i1, 1.0, 0.0)
    oh = oh0 + oh1
    before = jnp.dot(lt_ref[...], oh.astype(BF16), preferred_element_type=F32)
    rank0 = jnp.sum(before * oh0, axis=-1, keepdims=True)
    rank1 = jnp.sum(before * oh1, axis=-1, keepdims=True)
    out = jnp.where(lane == 0, i0, 0.0)
    out = jnp.where(lane == 1, i1, out)
    out = jnp.where(lane == 2, w0, out)
    out = jnp.where(lane == 3, w1, out)
    out = jnp.where(lane == 4, rank0, out)
    out = jnp.where(lane == 5, rank1, out)
    r_ref[...] = out
    c_ref[0] = jnp.sum(oh, axis=0, keepdims=True)


def router(x, y, wo, g, w_router, b_router, *, tm):
    n, d = x.shape
    nt = n // tm
    wp = jnp.pad(w_router, ((0, 0), (0, LANES - N_EXPERTS)))
    wh = wp.astype(BF16)
    wl = (wp - wh.astype(F32)).astype(BF16)
    bp = jnp.pad(b_router, (0, LANES - N_EXPERTS)).reshape(1, LANES)
    t = jnp.arange(tm)
    lower = (t[None, :] < t[:, None]).astype(BF16)
    const = lambda shp: pl.BlockSpec(shp, lambda i: (0, 0))
    return pl.pallas_call(
        _router_kernel,
        grid=(nt,),
        in_specs=[pl.BlockSpec((tm, d), lambda i: (i, 0)), pl.BlockSpec((tm, y.shape[1]), lambda i: (i, 0)),
                  const(wo.shape), const((1, d)), const((d, LANES)),
                  const((d, LANES)), const((1, LANES)), const((tm, tm))],
        out_specs=[pl.BlockSpec((tm, d), lambda i: (i, 0)), pl.BlockSpec((tm, d), lambda i: (i, 0)),
                   pl.BlockSpec((tm, LANES), lambda i: (i, 0)), pl.BlockSpec((1, 1, LANES), lambda i: (i, 0, 0))],
        out_shape=[jax.ShapeDtypeStruct((n, d), F32), jax.ShapeDtypeStruct((n, d), BF16),
                   jax.ShapeDtypeStruct((n, LANES), F32), jax.ShapeDtypeStruct((nt, 1, LANES), F32)],
        compiler_params=_cparams(("parallel",)),
        name="router",
    )(x, y, wo, g.reshape(1, d), wh, wl, bp, lower)


GRAN = 8


def _run_rows(tm):
    return -(-(2 * tm + N_EXPERTS * GRAN) // LANES) * LANES


def _slot_plan(counts, *, tm, tb):
    nt = counts.shape[0]
    n = nt * tm
    cnt = counts[:, 0, :N_EXPERTS].astype(jnp.int32)
    run = (cnt + GRAN - 1) // GRAN * GRAN
    local = jnp.cumsum(run, axis=1) - run
    total = jnp.sum(run, axis=0)
    padded = (total + tb - 1) // tb * tb
    pad_end = jnp.cumsum(padded)
    first = (pad_end - padded)[None, :] + jnp.cumsum(run, axis=0) - run
    local_row = jnp.pad(local.astype(F32), ((0, 0), (0, LANES - N_EXPERTS)))[:, None, :]
    n_slots = -(-(2 * n + nt * N_EXPERTS * GRAN + N_EXPERTS * tb) // tb) * tb
    blk_start = jnp.arange(n_slots // tb, dtype=jnp.int32) * tb
    block_e = jnp.minimum(jnp.sum(blk_start[:, None] >= pad_end[None, :], axis=1), N_EXPERTS - 1)
    n_used = (pad_end[N_EXPERTS - 1] // tb).reshape(1)
    tables = tuple(a.reshape(-1).astype(jnp.int32) for a in (first, run // GRAN, local))
    return (tables, local_row, block_e.astype(jnp.int32), n_used.astype(jnp.int32),
            pad_end.astype(jnp.int32), n_slots)


def _run_positions(r, local_row):
    lane = lax.broadcasted_iota(jnp.int32, r.shape, 1).astype(F32)
    start = lambda col: jnp.sum(jnp.where(lane == r[:, col:col + 1], local_row, 0.0), axis=-1, keepdims=True)
    return start(0) + r[:, 4:5], start(1) + r[:, 5:6]


def _run_copies(tables, step, make_copy, fn):
    first_ref, pieces_ref, local_ref = tables
    for e in range(N_EXPERTS):
        k = step * N_EXPERTS + e
        first = first_ref[k]
        local = local_ref[k]

        def piece(g, c, first=first, local=local):
            fn(make_copy(pl.multiple_of(first + g * GRAN, GRAN), pl.multiple_of(local + g * GRAN, GRAN)))
            return c
        lax.fori_loop(0, pieces_ref[k], piece, 0)


def _dispatch_kernel(first_ref, pieces_ref, local_ref, pe_ref, h_ref, r_ref, lrow_ref, xb_hbm, cbuf, zbuf,
                     sem_r, sem_z, *, tb):
    i = pl.program_id(0)
    n = pl.num_programs(0)
    slot = i % 2
    tables = (first_ref, pieces_ref, local_ref)
    d = h_ref.shape[1]

    def copies(step, s, fn):
        _run_copies(tables, step, lambda row, loc: pltpu.make_async_copy(
            cbuf.at[s, pl.ds(loc, GRAN)], xb_hbm.at[pl.ds(row, GRAN)], sem_r.at[s]), fn)

    def zero_block(first):
        cp = pltpu.make_async_copy(zbuf, xb_hbm.at[pl.ds(pl.multiple_of(first, tb), tb)], sem_z.at[0])
        cp.start()
        cp.wait()

    @pl.when(i == 0)
    def _():
        zbuf[...] = jnp.zeros_like(zbuf)
        for e in range(N_EXPERTS):
            end = pe_ref[e]
            start = pe_ref[e - 1] if e > 0 else 0
            pl.when(end > start)(functools.partial(zero_block, end - tb))
            tail = pe_ref[N_EXPERTS - 1] + e * tb
            pl.when(tail < xb_hbm.shape[0])(functools.partial(zero_block, tail))

    def step(s):
        @pl.when(i >= 2)
        def _():
            copies(i - 2, s, lambda c: c.wait())

        cr, tm = cbuf.shape[1], h_ref.shape[0]
        r = r_ref[...]
        p0, p1 = _run_positions(r, lrow_ref[...])
        lane = lax.broadcasted_iota(jnp.int32, r.shape, 1)
        t = jnp.where(lane == 4, p0, jnp.where(lane == 5, p1, r)).T
        pos0, pos1, w0, w1 = t[4:5, :], t[5:6, :], t[2:3, :], t[3:4, :]
        j = lax.broadcasted_iota(jnp.int32, (cr, tm), 0).astype(F32)
        m0 = j == pos0
        m1 = j == pos1
        pick = jnp.where(jnp.logical_or(m0, m1), 1.0, 0.0).astype(BF16)
        cbuf[s, :, 0:d] = jnp.dot(pick, h_ref[...], preferred_element_type=F32)
        wsel = jnp.sum(jnp.where(m0, w0, 0.0) + jnp.where(m1, w1, 0.0), axis=1, keepdims=True)
        cbuf[s, :, d:d + LANES] = jnp.broadcast_to(wsel, (cr, LANES))
        copies(i, s, lambda c: c.start())

        @pl.when(i == n - 1)
        def _():
            copies(i, s, lambda c: c.wait())

            @pl.when(n > 1)
            def _():
                copies(i - 1, 1 - s, lambda c: c.wait())

    _for_slot(slot, step)


def dispatch(hp, r, local_row, tables, pad_end, n_slots, *, tm, tb):
    n, d = hp.shape
    cr = _run_rows(tm)
    grid_spec = pltpu.PrefetchScalarGridSpec(
        num_scalar_prefetch=4,
        grid=(n // tm,),
        in_specs=[pl.BlockSpec((tm, d), lambda i, *_: (i, 0)),
                  pl.BlockSpec((tm, LANES), lambda i, *_: (i, 0)),
                  pl.BlockSpec((None, 1, LANES), lambda i, *_: (i, 0, 0))],
        out_specs=pl.BlockSpec(memory_space=pl.ANY),
        scratch_shapes=[pltpu.VMEM((2, cr, d + LANES), F32), pltpu.VMEM((tb, d + LANES), F32),
                        pltpu.SemaphoreType.DMA((2,)), pltpu.SemaphoreType.DMA((1,))],
    )
    return pl.pallas_call(
        functools.partial(_dispatch_kernel, tb=tb),
        grid_spec=grid_spec,
        out_shape=jax.ShapeDtypeStruct((n_slots, d + LANES), F32),
        compiler_params=_cparams(("arbitrary",)),
        name="dispatch",
    )(*tables, pad_end, hp, r, local_row)


def _moe_kernel(be_ref, nu_ref, x_ref, wg_hbm, wu_hbm, wd_hbm, o_ref, wg, wu, wd, sem_w, *, fc):
    i = pl.program_id(0)

    @pl.when(i < nu_ref[0])
    def _():
        e = be_ref[i]
        changed = jnp.logical_or(i == 0, e != be_ref[jnp.maximum(i - 1, 0)])

        @pl.when(changed)
        def _():
            copies = [pltpu.make_async_copy(src.at[e], dst, sem_w.at[j])
                      for j, (src, dst) in enumerate(((wg_hbm, wg), (wu_hbm, wu), (wd_hbm, wd)))]
            for c in copies:
                c.start()
            for c in copies:
                c.wait()

        d = o_ref.shape[1]
        x = x_ref[:, 0:d].astype(BF16)
        d_exp = wg.shape[1]
        acc = jnp.zeros(o_ref.shape, F32)
        for c0 in range(0, d_exp, fc):
            gate = jnp.dot(x, wg[:, c0:c0 + fc], preferred_element_type=F32)
            up = jnp.dot(x, wu[:, c0:c0 + fc], preferred_element_type=F32)
            act = (gate * jax.nn.sigmoid(gate) * up).astype(BF16)
            acc = acc + jnp.dot(act, wd[c0:c0 + fc, :], preferred_element_type=F32)
        o_ref[...] = acc * x_ref[:, d:d + 1]

    @pl.when(i >= nu_ref[0])
    def _():
        o_ref[...] = jnp.zeros_like(o_ref)


def moe_experts(xb, block_e, n_used, wg, wu, wd, *, tb, fc):
    n_slots, dp = xb.shape
    d = dp - LANES
    f = wg.shape[2]
    blk = lambda i, be, nu: (jnp.maximum(jnp.minimum(i, nu[0] - 1), 0), 0)
    grid_spec = pltpu.PrefetchScalarGridSpec(
        num_scalar_prefetch=2,
        grid=(n_slots // tb,),
        in_specs=[pl.BlockSpec((tb, dp), blk),
                  pl.BlockSpec(memory_space=pl.ANY), pl.BlockSpec(memory_space=pl.ANY),
                  pl.BlockSpec(memory_space=pl.ANY)],
        out_specs=pl.BlockSpec((tb, d), lambda i, be, nu: (i, 0)),
        scratch_shapes=[pltpu.VMEM((d, f), BF16), pltpu.VMEM((d, f), BF16), pltpu.VMEM((f, d), BF16),
                        pltpu.SemaphoreType.DMA((3,))],
    )
    return pl.pallas_call(
        functools.partial(_moe_kernel, fc=fc),
        grid_spec=grid_spec,
        out_shape=jax.ShapeDtypeStruct((n_slots, d), F32),
        compiler_params=_cparams(("arbitrary",)),
        name="moe_experts",
    )(block_e, n_used, xb, wg, wu, wd)


def _combine_kernel(first_ref, pieces_ref, local_ref, x_ref, r_ref, lrow_ref, g_ref, yb_hbm, o_ref, ybuf, sem_g):
    i = pl.program_id(0)
    n = pl.num_programs(0)
    slot = i % 2
    tables = (first_ref, pieces_ref, local_ref)

    def copies(step, s, fn):
        _run_copies(tables, step, lambda row, loc: pltpu.make_async_copy(
            yb_hbm.at[pl.ds(row, GRAN)], ybuf.at[s, pl.ds(loc, GRAN)], sem_g.at[s]), fn)

    @pl.when(i == 0)
    def _():
        ybuf[...] = jnp.zeros_like(ybuf)
        copies(0, 0, lambda c: c.start())

    def step(s):
        @pl.when(i + 1 < n)
        def _():
            copies(i + 1, 1 - s, lambda c: c.start())

        copies(i, s, lambda c: c.wait())
        tm, cr = x_ref.shape[0], ybuf.shape[1]
        p0, p1 = _run_positions(r_ref[...], lrow_ref[...])
        j = lax.broadcasted_iota(jnp.int32, (tm, cr), 1).astype(F32)
        pick = jnp.where(jnp.logical_or(j == p0, j == p1), 1.0, 0.0).astype(BF16)
        y = jnp.dot(pick, ybuf[s].astype(BF16), preferred_element_type=F32)
        o_ref[...] = _rms(x_ref[...] + y, g_ref[...])

    _for_slot(slot, step)


def combine_final(x, r, local_row, yb, tables, g, *, tm):
    n, d = x.shape
    grid_spec = pltpu.PrefetchScalarGridSpec(
        num_scalar_prefetch=3,
        grid=(n // tm,),
        in_specs=[pl.BlockSpec((tm, d), lambda i, *_: (i, 0)), pl.BlockSpec((tm, LANES), lambda i, *_: (i, 0)),
                  pl.BlockSpec((None, 1, LANES), lambda i, *_: (i, 0, 0)),
                  pl.BlockSpec((1, d), lambda i, *_: (0, 0)), pl.BlockSpec(memory_space=pl.ANY)],
        out_specs=pl.BlockSpec((tm, d), lambda i, *_: (i, 0)),
        scratch_shapes=[pltpu.VMEM((2, _run_rows(tm), d), F32), pltpu.SemaphoreType.DMA((2,))],
    )
    return pl.pallas_call(
        _combine_kernel,
        grid_spec=grid_spec,
        out_shape=jax.ShapeDtypeStruct((n, d), F32),
        compiler_params=_cparams(("arbitrary",)),
        name="combine_final",
    )(*tables, x, r, local_row, g.reshape(1, d), yb)


def even_layer(x, g_mix, g_ffn, w_in, gn_w, w_out, wg, wu, wd, *, batch, t_len):
    proj = norm_proj(x, g_mix, w_in.astype(BF16), tm=512)
    y_ret = retention(proj, gn_w, batch=batch, t_len=t_len, blk=256)
    y_sb = stick_breaking(proj, batch=batch, t_len=t_len, tq=256, col0=4)
    half = y_ret.shape[1]
    wo = w_out.astype(BF16)
    return ffn(x, g_ffn, [y_ret, y_sb], [wo[:half], wo[half:]],
               wg.astype(BF16), wu.astype(BF16), wd.astype(BF16), tm=512, fc=1408)


def odd_layer_mixer(x, g_mix, w_in, conv_w, lru_conv_w, lru_conv_b, wa, ba, wx, bx, lam, *, batch, t_len):
    proj = norm_proj(x, g_mix, w_in.astype(BF16), tm=512)
    return lru_mixer(proj, conv_w, lru_conv_w, lru_conv_b, wa, ba, wx, bx, lam,
                     batch=batch, t_len=t_len, blk=256)


def moe_and_final(x, y, w_out, g_ffn, g_final, w_router, b_router, wg, wu, wd, *, tm=512, tb=512):
    x, hp, r, counts = router(x, y, w_out.astype(BF16), g_ffn, w_router, b_router, tm=tm)
    tables, local_row, block_e, n_used, pad_end, n_slots = _slot_plan(counts, tm=tm, tb=tb)
    xb = dispatch(hp, r, local_row, tables, pad_end, n_slots, tm=tm, tb=tb)
    yb = moe_experts(xb, block_e, n_used, wg.astype(BF16), wu.astype(BF16), wd.astype(BF16), tb=tb, fc=512)
    return combine_final(x, r, local_row, yb, tables, g_final, tm=tm)


def kernel(x, norm_mix, norm_ffn, norm_final, ev_w_in, ev_ret_gn, ev_w_out, ev_ffn_gate, ev_ffn_up,
           ev_ffn_down, od_w_in, od_conv_w, od_lru_conv_w, od_lru_conv_b, od_lru_wa, od_lru_ba,
           od_lru_wx, od_lru_bx, od_lru_lambda, od_w_out, od_router_w, od_router_b, od_exp_gate,
           od_exp_up, od_exp_down):
    batch, t_len, d = x.shape
    xf = x.reshape(batch * t_len, d)
    xf = even_layer(xf, norm_mix[0], norm_ffn[0], ev_w_in[0], ev_ret_gn[0], ev_w_out[0],
                    ev_ffn_gate[0], ev_ffn_up[0], ev_ffn_down[0], batch=batch, t_len=t_len)
    y = odd_layer_mixer(xf, norm_mix[1], od_w_in[0], od_conv_w[0], od_lru_conv_w[0], od_lru_conv_b[0],
                        od_lru_wa[0], od_lru_ba[0], od_lru_wx[0], od_lru_bx[0], od_lru_lambda[0],
                        batch=batch, t_len=t_len)
    out = moe_and_final(xf, y, od_w_out[0], norm_ffn[1], norm_final, od_router_w[0], od_router_b[0],
                        od_exp_gate[0], od_exp_up[0], od_exp_down[0])
    return out.reshape(batch, t_len, d)
```

```python
import functools
import math

import jax
import jax.numpy as jnp
from jax import lax
from jax.experimental import pallas as pl
from jax.experimental.pallas import tpu as pltpu

F32 = jnp.float32
BF16 = jnp.bfloat16

EPS = 1e-6
CHUNK = 64
RET_HEADS = 4
RET_D = 128
SB_PAIRS = 4
SB_DH = 64
ROPE_BASE = 10000.0
CONV_W = 512
LRU_W = 512
LRU_C = 8.0
N_EXPERTS = 8
LANES = 128
VMEM_LIMIT = 56 * 1024 * 1024
SB_DEAD = 104.0


def _cparams(sem):
    return pltpu.CompilerParams(dimension_semantics=sem, vmem_limit_bytes=VMEM_LIMIT)


def _rms(x, g):
    return x * lax.rsqrt(jnp.mean(x * x, axis=-1, keepdims=True) + EPS) * g


def _for_slot(slot, fn):
    for s in range(2):
        pl.when(slot == s)(functools.partial(fn, s))


def _norm_proj_kernel(x_ref, g_ref, w_ref, o_ref):
    h = _rms(x_ref[...], g_ref[...]).astype(BF16)
    o_ref[...] = jnp.dot(h, w_ref[...], preferred_element_type=F32).astype(o_ref.dtype)


def norm_proj(x, g, w, *, tm):
    n, d = x.shape
    m = w.shape[1]
    return pl.pallas_call(
        _norm_proj_kernel,
        grid=(n // tm,),
        in_specs=[
            pl.BlockSpec((tm, d), lambda i: (i, 0)),
            pl.BlockSpec((1, d), lambda i: (0, 0)),
            pl.BlockSpec((d, m), lambda i: (0, 0)),
        ],
        out_specs=pl.BlockSpec((tm, m), lambda i: (i, 0)),
        out_shape=jax.ShapeDtypeStruct((n, m), BF16),
        compiler_params=_cparams(("parallel",)),
        name="norm_proj",
    )(x, g.reshape(1, d), w)


def _retention_kernel(q_ref, k_ref, v_ref, g_ref, cos_ref, sin_ref, dmat_ref, qdec_ref, kdec_ref,
                      sdec_ref, gn_ref, o_ref, s_ref):
    @pl.when(pl.program_id(1) == 0)
    def _():
        s_ref[...] = jnp.zeros_like(s_ref)

    cos = cos_ref[...]
    sin = sin_ref[...]
    half = RET_D // 2
    for h in range(RET_HEADS):
        cols = slice(h * RET_D, (h + 1) * RET_D)
        q = q_ref[:, cols].astype(F32)
        k = k_ref[:, cols].astype(F32)
        q = q * cos + pltpu.roll(q, half, 1) * sin
        k = (k * cos + pltpu.roll(k, half, 1) * sin) * (RET_D ** -0.5)
        qb = q.astype(BF16)
        kb = k.astype(BF16)
        v = v_ref[:, cols]

        s = lax.dot_general(qb, kb, (((1,), (1,)), ((), ())), preferred_element_type=F32)
        s = s * dmat_ref[h]
        o = jnp.dot(s.astype(BF16), v, preferred_element_type=F32)
        state = s_ref[h]
        o = o + jnp.dot(qb, state.astype(BF16), preferred_element_type=F32) * qdec_ref[h]

        kd = (k * kdec_ref[h]).astype(BF16)
        kv = lax.dot_general(kd, v, (((0,), (0,)), ((), ())), preferred_element_type=F32)
        s_ref[h] = state * sdec_ref[h] + kv

        mu = jnp.mean(o, axis=-1, keepdims=True)
        oc = o - mu
        var = jnp.mean(oc * oc, axis=-1, keepdims=True)
        on = oc * lax.rsqrt(var + EPS)
        gate = g_ref[:, cols].astype(F32)
        o_ref[:, cols] = (on * gn_ref[h] * (gate * jax.nn.sigmoid(gate))).astype(o_ref.dtype)


def _retention_tables(t_len, blk):
    h = jnp.arange(RET_HEADS, dtype=F32)
    log_g = jnp.log(1.0 - 2.0 ** (-5.0 - h))
    idx = jnp.arange(blk, dtype=F32)
    diff = idx[:, None] - idx[None, :]
    ci = jnp.arange(blk)[:, None] // CHUNK
    cj = jnp.arange(blk)[None, :] // CHUNK
    expo = jnp.where(ci == cj, jnp.abs(diff), diff)
    dmat = jnp.where(cj <= ci, jnp.exp(log_g[:, None, None] * expo[None]), 0.0)
    qdec = jnp.exp(log_g[:, None] * (idx + 1.0)[None, :])
    kdec = jnp.exp(log_g[:, None] * (blk - 1.0 - idx)[None, :])
    sdec = jnp.exp(log_g * blk)
    bc = lambda a: jnp.broadcast_to(a[:, :, None], (RET_HEADS, blk, RET_D))
    sdec = jnp.broadcast_to(sdec[:, None, None], (RET_HEADS, 1, RET_D))
    inv_freq = 1.0 / (ROPE_BASE ** (jnp.arange(0, RET_D, 2, dtype=F32) / RET_D))
    ang = jnp.arange(t_len, dtype=F32)[:, None] * inv_freq[None, :]
    cos = jnp.cos(ang)
    sin = jnp.sin(ang)
    cos2 = jnp.concatenate([cos, cos], axis=-1)
    sin2 = jnp.concatenate([-sin, sin], axis=-1)
    return cos2, sin2, dmat, bc(qdec), bc(kdec), sdec


def retention(proj, gn_w, *, batch, t_len, blk):
    n = proj.shape[0]
    nt = t_len // blk
    w = RET_HEADS * RET_D
    cos2, sin2, dmat, qdec, kdec, sdec = _retention_tables(t_len, blk)
    gn = gn_w.reshape(RET_HEADS, 1, RET_D)
    col = lambda c: pl.BlockSpec((blk, w), lambda b, t, c=c: (b * nt + t, c))
    tab_t = pl.BlockSpec((blk, RET_D), lambda b, t: (t, 0))
    whole = lambda a: pl.BlockSpec(a.shape, lambda b, t: (0, 0, 0))
    return pl.pallas_call(
        _retention_kernel,
        grid=(batch, nt),
        in_specs=[col(0), col(1), col(2), col(3), tab_t, tab_t,
                  whole(dmat), whole(qdec), whole(kdec), whole(sdec), whole(gn)],
        out_specs=pl.BlockSpec((blk, w), lambda b, t: (b * nt + t, 0)),
        out_shape=jax.ShapeDtypeStruct((n, w), BF16),
        scratch_shapes=[pltpu.VMEM((RET_HEADS, RET_D, RET_D), F32)],
        compiler_params=_cparams(("parallel", "arbitrary")),
        name="retention",
    )(proj, proj, proj, proj, cos2, sin2, dmat, qdec, kdec, sdec, gn)


def _softplus(z):
    return jnp.maximum(z, 0.0) + jnp.log(1.0 + jnp.exp(-jnp.abs(z)))


def _sb_kernel(q_ref, k_ref, v_ref, tri_ref, o_ref, carry_ref, acc_ref, qs_ref, *, tq):
    qi = pl.program_id(1)
    lane = lax.broadcasted_iota(jnp.int32, (1, LANES), 1)
    lo_lanes = lane < SB_DH
    tri2 = tri_ref[...]

    n_heads = 2 * SB_PAIRS
    for p in range(SB_PAIRS):
        q = q_ref[:, p * LANES:(p + 1) * LANES].astype(F32) * (SB_DH ** -0.5)
        qs_ref[(2 * p) * tq:(2 * p + 1) * tq, :] = jnp.where(lo_lanes, q, 0.0).astype(BF16)
        qs_ref[(2 * p + 1) * tq:(2 * p + 2) * tq, :] = jnp.where(lo_lanes, 0.0, q).astype(BF16)

    def tile(kv_start, mask):
        z = jnp.concatenate([
            lax.dot_general(qs_ref[2 * p * tq:(2 * p + 2) * tq, :], k_ref[pl.ds(kv_start, tq), p * LANES:(p + 1) * LANES],
                            (((1,), (1,)), ((), ())), preferred_element_type=F32)
            for p in range(SB_PAIRS)], axis=0)
        sp = _softplus(z)
        if mask is not None:
            mask = jnp.concatenate([mask] * n_heads, axis=0)
            sp = jnp.where(mask, sp, 0.0)
        hi = sp.astype(BF16)
        lo = (sp - hi.astype(F32)).astype(BF16)
        sums = jnp.dot(jnp.concatenate([hi, lo], axis=1), tri2, preferred_element_type=F32)
        carry = carry_ref[...]
        new_carry = carry + jnp.broadcast_to(sums[:, 0:1], carry.shape)
        carry_ref[...] = new_carry
        a = jnp.exp(z - sums - jnp.concatenate([carry] * (tq // LANES), axis=1))
        if mask is not None:
            a = jnp.where(mask, a, 0.0)
        a = a.astype(BF16)
        for p in range(SB_PAIRS):
            cols = slice(p * LANES, (p + 1) * LANES)
            pv = jnp.dot(a[2 * p * tq:(2 * p + 2) * tq, :], v_ref[pl.ds(kv_start, tq), cols],
                         preferred_element_type=F32)
            acc_ref[:, cols] += jnp.where(lo_lanes, pv[0:tq], pv[tq:2 * tq])
        return jnp.min(new_carry)

    carry_ref[...] = jnp.zeros_like(carry_ref)
    acc_ref[...] = jnp.zeros_like(acc_ref)
    row = lax.broadcasted_iota(jnp.int32, (tq, tq), 0)
    colm = lax.broadcasted_iota(jnp.int32, (tq, tq), 1)
    least0 = tile(pl.multiple_of(qi * tq, tq), colm < row)

    def alive(state):
        jj, least = state
        return jnp.logical_and(jj < qi, least < SB_DEAD)

    def body(state):
        jj, _ = state
        return jj + 1, tile(pl.multiple_of((qi - 1 - jj) * tq, tq), None)

    lax.while_loop(alive, body, (jnp.int32(0), least0))
    o_ref[...] = acc_ref[...].astype(o_ref.dtype)


def stick_breaking(proj, *, batch, t_len, tq, col0):
    n = proj.shape[0]
    nq = t_len // tq
    w = SB_PAIRS * LANES
    r = jnp.arange(tq)
    tri = (r[:, None] >= r[None, :]).astype(BF16)
    tri2 = jnp.concatenate([tri, tri], axis=0)
    return pl.pallas_call(
        functools.partial(_sb_kernel, tq=tq),
        grid=(batch, nq),
        in_specs=[pl.BlockSpec((tq, w), lambda b, i: (b * nq + i, col0)),
                  pl.BlockSpec((t_len, w), lambda b, i: (b, col0 + 1)),
                  pl.BlockSpec((t_len, w), lambda b, i: (b, col0 + 2)),
                  pl.BlockSpec(tri2.shape, lambda b, i: (0, 0))],
        out_specs=pl.BlockSpec((tq, w), lambda b, i: (b * nq + i, 0)),
        out_shape=jax.ShapeDtypeStruct((n, w), BF16),
        scratch_shapes=[pltpu.VMEM((2 * SB_PAIRS * tq, LANES), F32), pltpu.VMEM((tq, w), F32),
                        pltpu.VMEM((2 * SB_PAIRS * tq, LANES), BF16)],
        compiler_params=_cparams(("parallel", "arbitrary")),
        name="stick_breaking",
    )(proj, proj, proj, tri2)


def _load_once(pairs, sem):
    @pl.when(pl.program_id(0) == 0)
    def _():
        copies = [pltpu.make_async_copy(src, dst, sem.at[j]) for j, (src, dst) in enumerate(pairs)]
        for c in copies:
            c.start()
        for c in copies:
            c.wait()


def _ffn_kernel(*refs, n_in, fc):
    x_ref, g_ref = refs[0], refs[1]
    y_refs = refs[2:2 + n_in]
    wo_refs = refs[2 + n_in:2 + 2 * n_in]
    wg_hbm, wu_hbm, wd_hbm, o_ref, wg, wu, wd, sem = refs[2 + 2 * n_in:]
    _load_once(((wg_hbm, wg), (wu_hbm, wu), (wd_hbm, wd)), sem)

    x1 = x_ref[...]
    for y_ref, wo_ref in zip(y_refs, wo_refs):
        x1 = x1 + jnp.dot(y_ref[...], wo_ref[...], preferred_element_type=F32)
    h = _rms(x1, g_ref[...]).astype(BF16)
    ff = jnp.zeros_like(x1)
    for c0 in range(0, wg.shape[1], fc):
        gate = jnp.dot(h, wg[:, c0:c0 + fc], preferred_element_type=F32)
        up = jnp.dot(h, wu[:, c0:c0 + fc], preferred_element_type=F32)
        act = (gate * jax.nn.sigmoid(gate) * up).astype(BF16)
        ff = ff + jnp.dot(act, wd[c0:c0 + fc, :], preferred_element_type=F32)
    o_ref[...] = x1 + ff


def ffn(x, g, ys, wos, wg, wu, wd, *, tm, fc):
    n, d = x.shape
    f = wg.shape[1]
    n_in = len(ys)
    hbm = pl.BlockSpec(memory_space=pl.ANY)
    in_specs = [pl.BlockSpec((tm, d), lambda i: (i, 0)), pl.BlockSpec((1, d), lambda i: (0, 0))]
    in_specs += [pl.BlockSpec((tm, y.shape[1]), lambda i: (i, 0)) for y in ys]
    in_specs += [pl.BlockSpec(w.shape, lambda i: (0, 0)) for w in wos]
    in_specs += [hbm, hbm, hbm]
    return pl.pallas_call(
        functools.partial(_ffn_kernel, n_in=n_in, fc=fc),
        grid=(n // tm,),
        in_specs=in_specs,
        out_specs=pl.BlockSpec((tm, d), lambda i: (i, 0)),
        out_shape=jax.ShapeDtypeStruct((n, d), F32),
        scratch_shapes=[pltpu.VMEM((d, f), BF16), pltpu.VMEM((d, f), BF16), pltpu.VMEM((f, d), BF16),
                        pltpu.SemaphoreType.DMA((3,))],
        compiler_params=_cparams(("arbitrary",)),
        name="ffn",
    )(x, g.reshape(1, d), *ys, *wos, wg, wu, wd)


def _shift_rows(cur, tail, d, row8):
    rolled = pltpu.roll(cur, d, 0)
    top = jnp.where(row8 < d, pltpu.roll(tail, d, 0), rolled[0:8])
    return jnp.concatenate([top, rolled[8:]], axis=0)


def _lru_kernel(p_ref, w3_ref, w4_ref, b4_ref, wa_ref, ba_ref, wx_ref, bx_ref, lam_ref, o_ref,
                cu_tail, xr_tail, h_prev, *, blk):
    @pl.when(pl.program_id(1) == 0)
    def _():
        cu_tail[...] = jnp.zeros_like(cu_tail)
        xr_tail[...] = jnp.zeros_like(xr_tail)
        h_prev[...] = jnp.zeros_like(h_prev)

    w = CONV_W
    row8 = lax.broadcasted_iota(jnp.int32, (8, w), 0)
    gb = p_ref[:, 0:w].astype(F32)
    cu = p_ref[:, w:2 * w].astype(F32) * p_ref[:, 2 * w:3 * w].astype(F32)
    xr = p_ref[:, 3 * w:4 * w].astype(F32)
    xg = p_ref[:, 4 * w:5 * w].astype(F32)

    ct = cu_tail[...]
    conv3 = (w3_ref[2:3, :] * cu + w3_ref[1:2, :] * _shift_rows(cu, ct, 1, row8)
             + w3_ref[0:1, :] * _shift_rows(cu, ct, 2, row8))
    cu_tail[...] = cu[blk - 8:blk]
    o_ref[:, 0:w] = (gb * conv3).astype(o_ref.dtype)

    xt = xr_tail[...]
    xc = (w4_ref[3:4, :] * xr + w4_ref[2:3, :] * _shift_rows(xr, xt, 1, row8)
          + w4_ref[1:2, :] * _shift_rows(xr, xt, 2, row8)
          + w4_ref[0:1, :] * _shift_rows(xr, xt, 3, row8)) + b4_ref[...]
    xr_tail[...] = xr[blk - 8:blk]

    xcb = xc.astype(BF16)
    r = jax.nn.sigmoid(jnp.dot(xcb, wa_ref[...], preferred_element_type=F32) + ba_ref[...])
    ig = jax.nn.sigmoid(jnp.dot(xcb, wx_ref[...], preferred_element_type=F32) + bx_ref[...])
    log_a = (-LRU_C * r) * _softplus(-lam_ref[...])
    a = jnp.exp(log_a)
    b = jnp.sqrt(1.0 - a * a) * (ig * xc)

    row = lax.broadcasted_iota(jnp.int32, (blk, w), 0)
    d = 1
    while d < blk:
        keep = row >= d
        a_s = jnp.where(keep, pltpu.roll(a, d, 0), 1.0)
        b_s = jnp.where(keep, pltpu.roll(b, d, 0), 0.0)
        b = a * b_s + b
        a = a * a_s
        d *= 2
    h = a * h_prev[...] + b
    h_prev[...] = h[blk - 1:blk]

    c0 = math.sqrt(2.0 / math.pi)
    gelu = 0.5 * xg * (1.0 + jnp.tanh(c0 * (xg + 0.044715 * (xg * xg * xg))))
    o_ref[:, w:2 * w] = (h * gelu).astype(o_ref.dtype)


def _block_diag(wh):
    nh, bi, bo = wh.shape
    eye = jnp.eye(nh, dtype=wh.dtype)
    return (eye[:, None, :, None] * wh[:, :, None, :]).reshape(nh * bi, nh * bo)


def lru_mixer(proj, w3, w4, b4, wa, ba, wx, bx, lam, *, batch, t_len, blk):
    n, pw = proj.shape
    nt = t_len // blk
    w = CONV_W
    row = lambda a: a.reshape(1, w)
    full = lambda shp: pl.BlockSpec(shp, lambda b, t: (0, 0))
    return pl.pallas_call(
        functools.partial(_lru_kernel, blk=blk),
        grid=(batch, nt),
        in_specs=[pl.BlockSpec((blk, pw), lambda b, t: (b * nt + t, 0)),
                  full((8, w)), full((8, w)), full((1, w)),
                  full((w, w)), full((1, w)), full((w, w)), full((1, w)), full((1, w))],
        out_specs=pl.BlockSpec((blk, 2 * w), lambda b, t: (b * nt + t, 0)),
        out_shape=jax.ShapeDtypeStruct((n, 2 * w), BF16),
        scratch_shapes=[pltpu.VMEM((8, w), F32), pltpu.VMEM((8, w), F32), pltpu.VMEM((1, w), F32)],
        compiler_params=_cparams(("parallel", "arbitrary")),
        name="lru_mixer",
    )(proj, jnp.pad(w3, ((0, 8 - w3.shape[0]), (0, 0))), jnp.pad(w4, ((0, 8 - w4.shape[0]), (0, 0))),
      row(b4), _block_diag(wa).astype(BF16), row(ba), _block_diag(wx).astype(BF16), row(bx), row(lam))


def _router_kernel(x_ref, y_ref, wo_ref, g_ref, wh_ref, wl_ref, b_ref, lt_ref, x1_ref, hp_ref, r_ref, rt_ref,
                   c_ref):
    x1 = x_ref[...] + jnp.dot(y_ref[...], wo_ref[...], preferred_element_type=F32)
    x1_ref[...] = x1
    h = _rms(x1, g_ref[...])
    hh = h.astype(BF16)
    hp_ref[...] = hh
    hl = (h - hh.astype(F32)).astype(BF16)
    wh = wh_ref[...]
    logits = (jnp.dot(hh, wh, preferred_element_type=F32) + jnp.dot(hl, wh, preferred_element_type=F32)
              + jnp.dot(hh, wl_ref[...], preferred_element_type=F32)) + b_ref[...]
    lane = lax.broadcasted_iota(jnp.int32, logits.shape, 1).astype(F32)
    neg = jnp.float32(-jnp.inf)
    logits = jnp.where(lane < N_EXPERTS, logits, neg)
    m0 = jnp.max(logits, axis=-1, keepdims=True)
    i0 = jnp.min(jnp.where(logits == m0, lane, float(LANES)), axis=-1, keepdims=True)
    rest = jnp.where(lane == i0, neg, logits)
    m1 = jnp.max(rest, axis=-1, keepdims=True)
    i1 = jnp.min(jnp.where(rest == m1, lane, float(LANES)), axis=-1, keepdims=True)
    e1 = jnp.exp(m1 - m0)
    w0 = 1.0 / (1.0 + e1)
    w1 = e1 * w0
    oh0 = jnp.where(lane == i0, 1.0, 0.0)
    oh1 = jnp.where(lane == i1, 1.0, 0.0)
    oh = oh0 + oh1
    before = jnp.dot(lt_ref[...], oh.astype(BF16), preferred_element_type=F32)
    rank0 = jnp.sum(before * oh0, axis=-1, keepdims=True)
    rank1 = jnp.sum(before * oh1, axis=-1, keepdims=True)
    out = jnp.where(lane == 0, i0, 0.0)
    out = jnp.where(lane == 1, i1, out)
    out = jnp.where(lane == 2, w0, out)
    out = jnp.where(lane == 3, w1, out)
    out = jnp.where(lane == 4, rank0, out)
    out = jnp.where(lane == 5, rank1, out)
    r_ref[...] = out
    rt_ref[...] = out.T[0:8, :]
    c_ref[0] = jnp.sum(oh, axis=0, keepdims=True)


def router(x, y, wo, g, w_router, b_router, *, tm):
    n, d = x.shape
    nt = n // tm
    wp = jnp.pad(w_router, ((0, 0), (0, LANES - N_EXPERTS)))
    wh = wp.astype(BF16)
    wl = (wp - wh.astype(F32)).astype(BF16)
    bp = jnp.pad(b_router, (0, LANES - N_EXPERTS)).reshape(1, LANES)
    t = jnp.arange(tm)
    lower = (t[None, :] < t[:, None]).astype(BF16)
    const = lambda shp: pl.BlockSpec(shp, lambda i: (0, 0))
    return pl.pallas_call(
        _router_kernel,
        grid=(nt,),
        in_specs=[pl.BlockSpec((tm, d), lambda i: (i, 0)), pl.BlockSpec((tm, y.shape[1]), lambda i: (i, 0)),
                  const(wo.shape), const((1, d)), const((d, LANES)),
                  const((d, LANES)), const((1, LANES)), const((tm, tm))],
        out_specs=[pl.BlockSpec((tm, d), lambda i: (i, 0)), pl.BlockSpec((tm, d), lambda i: (i, 0)),
                   pl.BlockSpec((tm, LANES), lambda i: (i, 0)), pl.BlockSpec((None, 8, tm), lambda i: (i, 0, 0)),
                   pl.BlockSpec((1, 1, LANES), lambda i: (i, 0, 0))],
        out_shape=[jax.ShapeDtypeStruct((n, d), F32), jax.ShapeDtypeStruct((n, d), BF16),
                   jax.ShapeDtypeStruct((n, LANES), F32), jax.ShapeDtypeStruct((nt, 8, tm), F32),
                   jax.ShapeDtypeStruct((nt, 1, LANES), F32)],
        compiler_params=_cparams(("parallel",)),
        name="router",
    )(x, y, wo, g.reshape(1, d), wh, wl, bp, lower)


GRAN = 8


def _run_rows(tm):
    return -(-(2 * tm + N_EXPERTS * GRAN) // LANES) * LANES


def _slot_plan(counts, *, tm, tb):
    nt = counts.shape[0]
    n = nt * tm
    cnt = counts[:, 0, :N_EXPERTS].astype(jnp.int32)
    run = (cnt + GRAN - 1) // GRAN * GRAN
    local = jnp.cumsum(run, axis=1) - run
    total = jnp.sum(run, axis=0)
    padded = (total + tb - 1) // tb * tb
    pad_end = jnp.cumsum(padded)
    first = (pad_end - padded)[None, :] + jnp.cumsum(run, axis=0) - run
    local_mat = jnp.broadcast_to(jnp.pad(local, ((0, 0), (0, LANES - N_EXPERTS)))[:, :, None],
                                 (nt, LANES, LANES)).astype(BF16)
    n_slots = -(-(2 * n + nt * N_EXPERTS * GRAN + N_EXPERTS * tb) // tb) * tb
    blk_start = jnp.arange(n_slots // tb, dtype=jnp.int32) * tb
    block_e = jnp.minimum(jnp.sum(blk_start[:, None] >= pad_end[None, :], axis=1), N_EXPERTS - 1)
    n_used = (pad_end[N_EXPERTS - 1] // tb).reshape(1)
    tables = tuple(a.reshape(-1).astype(jnp.int32) for a in (first, run // GRAN, local))
    return (tables, local_mat, block_e.astype(jnp.int32), n_used.astype(jnp.int32),
            pad_end.astype(jnp.int32), n_slots)


def _run_positions(r, local_mat):
    lane = lax.broadcasted_iota(jnp.int32, r.shape, 1).astype(F32)

    def pos(col_e, col_rank):
        onehot = jnp.where(lane == r[:, col_e:col_e + 1], 1.0, 0.0).astype(BF16)
        return jnp.dot(onehot, local_mat, preferred_element_type=F32) + r[:, col_rank:col_rank + 1]
    return pos(0, 4), pos(1, 5)


def _run_copies(tables, step, make_copy, fn):
    first_ref, pieces_ref, local_ref = tables
    for e in range(N_EXPERTS):
        k = step * N_EXPERTS + e
        first = first_ref[k]
        local = local_ref[k]

        def piece(g, c, first=first, local=local):
            fn(make_copy(pl.multiple_of(first + g * GRAN, GRAN), pl.multiple_of(local + g * GRAN, GRAN)))
            return c
        lax.fori_loop(0, pieces_ref[k], piece, 0)


def _dispatch_kernel(first_ref, pieces_ref, local_ref, pe_ref, h_ref, rt_ref, xb_hbm, cbuf, zbuf,
                     sem_r, sem_z, *, tb):
    i = pl.program_id(0)
    n = pl.num_programs(0)
    slot = i % 2
    tables = (first_ref, pieces_ref, local_ref)
    d = h_ref.shape[1]

    def copies(step, s, fn):
        _run_copies(tables, step, lambda row, loc: pltpu.make_async_copy(
            cbuf.at[s, pl.ds(loc, GRAN)], xb_hbm.at[pl.ds(row, GRAN)], sem_r.at[s]), fn)

    def zero_block(first):
        cp = pltpu.make_async_copy(zbuf, xb_hbm.at[pl.ds(pl.multiple_of(first, tb), tb)], sem_z.at[0])
        cp.start()
        cp.wait()

    @pl.when(i == 0)
    def _():
        zbuf[...] = jnp.zeros_like(zbuf)
        for e in range(N_EXPERTS):
            end = pe_ref[e]
            start = pe_ref[e - 1] if e > 0 else 0
            pl.when(end > start)(functools.partial(zero_block, end - tb))
            tail = pe_ref[N_EXPERTS - 1] + e * tb
            pl.when(tail < xb_hbm.shape[0])(functools.partial(zero_block, tail))

    def step(s):
        @pl.when(i >= 2)
        def _():
            copies(i - 2, s, lambda c: c.wait())

        cr, tm = cbuf.shape[1], h_ref.shape[0]
        e0, e1, w0, w1 = rt_ref[0:1, :], rt_ref[1:2, :], rt_ref[2:3, :], rt_ref[3:4, :]
        pos0, pos1 = rt_ref[4:5, :], rt_ref[5:6, :]
        for e in range(N_EXPERTS):
            start = local_ref[i * N_EXPERTS + e].astype(F32)
            pos0 = pos0 + jnp.where(e0 == e, start, 0.0)
            pos1 = pos1 + jnp.where(e1 == e, start, 0.0)
        j = lax.broadcasted_iota(jnp.int32, (cr, tm), 0).astype(F32)
        m0 = j == pos0
        m1 = j == pos1
        pick = jnp.where(jnp.logical_or(m0, m1), 1.0, 0.0).astype(BF16)
        cbuf[s, :, 0:d] = jnp.dot(pick, h_ref[...], preferred_element_type=F32)
        wsel = jnp.sum(jnp.where(m0, w0, 0.0) + jnp.where(m1, w1, 0.0), axis=1, keepdims=True)
        cbuf[s, :, d:d + LANES] = jnp.broadcast_to(wsel, (cr, LANES))
        copies(i, s, lambda c: c.start())

        @pl.when(i == n - 1)
        def _():
            copies(i, s, lambda c: c.wait())

            @pl.when(n > 1)
            def _():
                copies(i - 1, 1 - s, lambda c: c.wait())

    _for_slot(slot, step)


def dispatch(hp, r_t, tables, pad_end, n_slots, *, tm, tb):
    n, d = hp.shape
    cr = _run_rows(tm)
    grid_spec = pltpu.PrefetchScalarGridSpec(
        num_scalar_prefetch=4,
        grid=(n // tm,),
        in_specs=[pl.BlockSpec((tm, d), lambda i, *_: (i, 0)),
                  pl.BlockSpec((None, 8, tm), lambda i, *_: (i, 0, 0))],
        out_specs=pl.BlockSpec(memory_space=pl.ANY),
        scratch_shapes=[pltpu.VMEM((2, cr, d + LANES), F32), pltpu.VMEM((tb, d + LANES), F32),
                        pltpu.SemaphoreType.DMA((2,)), pltpu.SemaphoreType.DMA((1,))],
    )
    return pl.pallas_call(
        functools.partial(_dispatch_kernel, tb=tb),
        grid_spec=grid_spec,
        out_shape=jax.ShapeDtypeStruct((n_slots, d + LANES), F32),
        compiler_params=_cparams(("arbitrary",)),
        name="dispatch",
    )(*tables, pad_end, hp, r_t)


def _moe_kernel(be_ref, nu_ref, x_ref, wg_hbm, wu_hbm, wd_hbm, o_ref, wg, wu, wd, sg, su, sd, sem, *, fc):
    i = pl.program_id(0)
    n_used = nu_ref[0]
    d = o_ref.shape[1]
    n_chunks = wg.shape[1] // fc

    def staged(e, c):
        s = c % 2
        cols = pl.ds(c * fc, fc)
        return (pltpu.make_async_copy(wg_hbm.at[e, :, cols], sg.at[s], sem.at[3 * s]),
                pltpu.make_async_copy(wu_hbm.at[e, :, cols], su.at[s], sem.at[3 * s + 1]),
                pltpu.make_async_copy(wd_hbm.at[e, cols, :], sd.at[s], sem.at[3 * s + 2]))

    def fetch(e, c):
        for cp in staged(e, c):
            cp.start()

    def install(e, c):
        for cp in staged(e, c):
            cp.wait()
        s = c % 2
        wg[:, c * fc:(c + 1) * fc] = sg[s].astype(BF16)
        wu[:, c * fc:(c + 1) * fc] = su[s].astype(BF16)
        wd[c * fc:(c + 1) * fc, :] = sd[s].astype(BF16)
        if c + 2 < n_chunks:
            fetch(e, c + 2)

    def compute(next_e):
        x = x_ref[:, 0:d].astype(BF16)
        acc = jnp.zeros(o_ref.shape, F32)
        for c in range(n_chunks):
            gate = jnp.dot(x, wg[:, c * fc:(c + 1) * fc], preferred_element_type=F32)
            up = jnp.dot(x, wu[:, c * fc:(c + 1) * fc], preferred_element_type=F32)
            act = (gate * jax.nn.sigmoid(gate) * up).astype(BF16)
            acc = acc + jnp.dot(act, wd[c * fc:(c + 1) * fc, :], preferred_element_type=F32)
            if next_e is not None:
                install(next_e, c)
        o_ref[...] = acc * x_ref[:, d:d + 1]

    @pl.when(i == 0)
    def _():
        e = be_ref[0]
        fetch(e, 0)
        fetch(e, 1)
        for c in range(n_chunks):
            install(e, c)

    is_last = jnp.logical_and(i + 1 < n_used, be_ref[jnp.minimum(i + 1, pl.num_programs(0) - 1)] != be_ref[i])

    @pl.when(jnp.logical_and(i < n_used, is_last))
    def _():
        next_e = be_ref[i + 1]
        fetch(next_e, 0)
        fetch(next_e, 1)
        compute(next_e)

    @pl.when(jnp.logical_and(i < n_used, jnp.logical_not(is_last)))
    def _():
        compute(None)

    @pl.when(i >= n_used)
    def _():
        o_ref[...] = jnp.zeros_like(o_ref)


def moe_experts(xb, block_e, n_used, wg, wu, wd, *, tb, fc):
    n_slots, dp = xb.shape
    d = dp - LANES
    f = wg.shape[2]
    blk = lambda i, be, nu: (jnp.maximum(jnp.minimum(i, nu[0] - 1), 0), 0)
    grid_spec = pltpu.PrefetchScalarGridSpec(
        num_scalar_prefetch=2,
        grid=(n_slots // tb,),
        in_specs=[pl.BlockSpec((tb, dp), blk),
                  pl.BlockSpec(memory_space=pl.ANY), pl.BlockSpec(memory_space=pl.ANY),
                  pl.BlockSpec(memory_space=pl.ANY)],
        out_specs=pl.BlockSpec((tb, d), lambda i, be, nu: (i, 0)),
        scratch_shapes=[pltpu.VMEM((d, f), BF16), pltpu.VMEM((d, f), BF16), pltpu.VMEM((f, d), BF16),
                        pltpu.VMEM((2, d, fc), F32), pltpu.VMEM((2, d, fc), F32), pltpu.VMEM((2, fc, d), F32),
                        pltpu.SemaphoreType.DMA((6,))],
    )
    return pl.pallas_call(
        functools.partial(_moe_kernel, fc=fc),
        grid_spec=grid_spec,
        out_shape=jax.ShapeDtypeStruct((n_slots, d), F32),
        compiler_params=_cparams(("arbitrary",)),
        name="moe_experts",
    )(block_e, n_used, xb, wg, wu, wd)


def _combine_kernel(first_ref, pieces_ref, local_ref, x_ref, r_ref, lmat_ref, g_ref, yb_hbm, o_ref, ybuf, sem_g):
    i = pl.program_id(0)
    n = pl.num_programs(0)
    slot = i % 2
    tables = (first_ref, pieces_ref, local_ref)

    def copies(step, s, fn):
        _run_copies(tables, step, lambda row, loc: pltpu.make_async_copy(
            yb_hbm.at[pl.ds(row, GRAN)], ybuf.at[s, pl.ds(loc, GRAN)], sem_g.at[s]), fn)

    @pl.when(i == 0)
    def _():
        ybuf[...] = jnp.zeros_like(ybuf)
        copies(0, 0, lambda c: c.start())

    def step(s):
        @pl.when(i + 1 < n)
        def _():
            copies(i + 1, 1 - s, lambda c: c.start())

        copies(i, s, lambda c: c.wait())
        tm, cr = x_ref.shape[0], ybuf.shape[1]
        p0, p1 = (jnp.concatenate([p] * (cr // LANES), axis=1) for p in _run_positions(r_ref[...], lmat_ref[...]))
        j = lax.broadcasted_iota(jnp.int32, (tm, cr), 1).astype(F32)
        pick = jnp.where(jnp.logical_or(j == p0, j == p1), 1.0, 0.0).astype(BF16)
        y = jnp.dot(pick, ybuf[s].astype(BF16), preferred_element_type=F32)
        o_ref[...] = _rms(x_ref[...] + y, g_ref[...])

    _for_slot(slot, step)


def combine_final(x, r, local_mat, yb, tables, g, *, tm):
    n, d = x.shape
    grid_spec = pltpu.PrefetchScalarGridSpec(
        num_scalar_prefetch=3,
        grid=(n // tm,),
        in_specs=[pl.BlockSpec((tm, d), lambda i, *_: (i, 0)), pl.BlockSpec((tm, LANES), lambda i, *_: (i, 0)),
                  pl.BlockSpec((None, LANES, LANES), lambda i, *_: (i, 0, 0)),
                  pl.BlockSpec((1, d), lambda i, *_: (0, 0)), pl.BlockSpec(memory_space=pl.ANY)],
        out_specs=pl.BlockSpec((tm, d), lambda i, *_: (i, 0)),
        scratch_shapes=[pltpu.VMEM((2, _run_rows(tm), d), F32), pltpu.SemaphoreType.DMA((2,))],
    )
    return pl.pallas_call(
        _combine_kernel,
        grid_spec=grid_spec,
        out_shape=jax.ShapeDtypeStruct((n, d), F32),
        compiler_params=_cparams(("arbitrary",)),
        name="combine_final",
    )(*tables, x, r, local_mat, g.reshape(1, d), yb)


def even_layer(x, g_mix, g_ffn, w_in, gn_w, w_out, wg, wu, wd, *, batch, t_len):
    proj = norm_proj(x, g_mix, w_in.astype(BF16), tm=512)
    y_ret = retention(proj, gn_w, batch=batch, t_len=t_len, blk=256)
    y_sb = stick_breaking(proj, batch=batch, t_len=t_len, tq=256, col0=4)
    half = y_ret.shape[1]
    wo = w_out.astype(BF16)
    return ffn(x, g_ffn, [y_ret, y_sb], [wo[:half], wo[half:]],
               wg.astype(BF16), wu.astype(BF16), wd.astype(BF16), tm=512, fc=1408)


def odd_layer_mixer(x, g_mix, w_in, conv_w, lru_conv_w, lru_conv_b, wa, ba, wx, bx, lam, *, batch, t_len):
    proj = norm_proj(x, g_mix, w_in.astype(BF16), tm=512)
    return lru_mixer(proj, conv_w, lru_conv_w, lru_conv_b, wa, ba, wx, bx, lam,
                     batch=batch, t_len=t_len, blk=256)


def moe_and_final(x, y, w_out, g_ffn, g_final, w_router, b_router, wg, wu, wd, *, tm=512, tb=512):
    x, hp, r, r_t, counts = router(x, y, w_out.astype(BF16), g_ffn, w_router, b_router, tm=tm)
    tables, local_mat, block_e, n_used, pad_end, n_slots = _slot_plan(counts, tm=tm, tb=tb)
    xb = dispatch(hp, r_t, tables, pad_end, n_slots, tm=tm, tb=tb)
    yb = moe_experts(xb, block_e, n_used, wg, wu, wd, tb=tb, fc=512)
    return combine_final(x, r, local_mat, yb, tables, g_final, tm=tm)


def kernel(x, norm_mix, norm_ffn, norm_final, ev_w_in, ev_ret_gn, ev_w_out, ev_ffn_gate, ev_ffn_up,
           ev_ffn_down, od_w_in, od_conv_w, od_lru_conv_w, od_lru_conv_b, od_lru_wa, od_lru_ba,
           od_lru_wx, od_lru_bx, od_lru_lambda, od_w_out, od_router_w, od_router_b, od_exp_gate,
           od_exp_up, od_exp_down):
    batch, t_len, d = x.shape
    xf = x.reshape(batch * t_len, d)
    xf = even_layer(xf, norm_mix[0], norm_ffn[0], ev_w_in[0], ev_ret_gn[0], ev_w_out[0],
                    ev_ffn_gate[0], ev_ffn_up[0], ev_ffn_down[0], batch=batch, t_len=t_len)
    y = odd_layer_mixer(xf, norm_mix[1], od_w_in[0], od_conv_w[0], od_lru_conv_w[0], od_lru_conv_b[0],
                        od_lru_wa[0], od_lru_ba[0], od_lru_wx[0], od_lru_bx[0], od_lru_lambda[0],
                        batch=batch, t_len=t_len)
    out = moe_and_final(xf, y, od_w_out[0], norm_ffn[1], norm_final, od_router_w[0], od_router_b[0],
                        od_exp_gate[0], od_exp_up[0], od_exp_down[0])
    return out.reshape(batch, t_len, d)
```

```python
import functools
import math

import jax
import jax.numpy as jnp
from jax import lax
from jax.experimental import pallas as pl
from jax.experimental.pallas import tpu as pltpu

F32 = jnp.float32
BF16 = jnp.bfloat16

EPS = 1e-6
CHUNK = 64
RET_HEADS = 4
RET_D = 128
SB_PAIRS = 4
SB_DH = 64
ROPE_BASE = 10000.0
CONV_W = 512
LRU_W = 512
LRU_C = 8.0
N_EXPERTS = 8
LANES = 128
VMEM_LIMIT = 56 * 1024 * 1024
SB_DEAD = 104.0


def _cparams(sem):
    return pltpu.CompilerParams(dimension_semantics=sem, vmem_limit_bytes=VMEM_LIMIT)


def _rms(x, g):
    return x * lax.rsqrt(jnp.mean(x * x, axis=-1, keepdims=True) + EPS) * g


def _for_slot(slot, fn):
    for s in range(2):
        pl.when(slot == s)(functools.partial(fn, s))


def _norm_proj_kernel(x_ref, g_ref, w_ref, o_ref):
    h = _rms(x_ref[...], g_ref[...]).astype(BF16)
    o_ref[...] = jnp.dot(h, w_ref[...], preferred_element_type=F32).astype(o_ref.dtype)


def norm_proj(x, g, w, *, tm):
    n, d = x.shape
    m = w.shape[1]
    return pl.pallas_call(
        _norm_proj_kernel,
        grid=(n // tm,),
        in_specs=[
            pl.BlockSpec((tm, d), lambda i: (i, 0)),
            pl.BlockSpec((1, d), lambda i: (0, 0)),
            pl.BlockSpec((d, m), lambda i: (0, 0)),
        ],
        out_specs=pl.BlockSpec((tm, m), lambda i: (i, 0)),
        out_shape=jax.ShapeDtypeStruct((n, m), BF16),
        compiler_params=_cparams(("parallel",)),
        name="norm_proj",
    )(x, g.reshape(1, d), w)


def _retention_kernel(q_ref, k_ref, v_ref, g_ref, cos_ref, sin_ref, dmat_ref, qdec_ref, kdec_ref,
                      sdec_ref, gn_ref, o_ref, s_ref):
    @pl.when(pl.program_id(1) == 0)
    def _():
        s_ref[...] = jnp.zeros_like(s_ref)

    cos = cos_ref[...]
    sin = sin_ref[...]
    half = RET_D // 2
    for h in range(RET_HEADS):
        cols = slice(h * RET_D, (h + 1) * RET_D)
        q = q_ref[:, cols].astype(F32)
        k = k_ref[:, cols].astype(F32)
        q = q * cos + pltpu.roll(q, half, 1) * sin
        k = (k * cos + pltpu.roll(k, half, 1) * sin) * (RET_D ** -0.5)
        qb = q.astype(BF16)
        kb = k.astype(BF16)
        v = v_ref[:, cols]

        s = lax.dot_general(qb, kb, (((1,), (1,)), ((), ())), preferred_element_type=F32)
        s = s * dmat_ref[h]
        o = jnp.dot(s.astype(BF16), v, preferred_element_type=F32)
        state = s_ref[h]
        o = o + jnp.dot(qb, state.astype(BF16), preferred_element_type=F32) * qdec_ref[h]

        kd = (k * kdec_ref[h]).astype(BF16)
        kv = lax.dot_general(kd, v, (((0,), (0,)), ((), ())), preferred_element_type=F32)
        s_ref[h] = state * sdec_ref[h] + kv

        mu = jnp.mean(o, axis=-1, keepdims=True)
        oc = o - mu
        var = jnp.mean(oc * oc, axis=-1, keepdims=True)
        on = oc * lax.rsqrt(var + EPS)
        gate = g_ref[:, cols].astype(F32)
        o_ref[:, cols] = (on * gn_ref[h] * (gate * jax.nn.sigmoid(gate))).astype(o_ref.dtype)


def _retention_tables(t_len, blk):
    h = jnp.arange(RET_HEADS, dtype=F32)
    log_g = jnp.log(1.0 - 2.0 ** (-5.0 - h))
    idx = jnp.arange(blk, dtype=F32)
    diff = idx[:, None] - idx[None, :]
    ci = jnp.arange(blk)[:, None] // CHUNK
    cj = jnp.arange(blk)[None, :] // CHUNK
    expo = jnp.where(ci == cj, jnp.abs(diff), diff)
    dmat = jnp.where(cj <= ci, jnp.exp(log_g[:, None, None] * expo[None]), 0.0)
    qdec = jnp.exp(log_g[:, None] * (idx + 1.0)[None, :])
    kdec = jnp.exp(log_g[:, None] * (blk - 1.0 - idx)[None, :])
    sdec = jnp.exp(log_g * blk)
    bc = lambda a: jnp.broadcast_to(a[:, :, None], (RET_HEADS, blk, RET_D))
    sdec = jnp.broadcast_to(sdec[:, None, None], (RET_HEADS, 1, RET_D))
    inv_freq = 1.0 / (ROPE_BASE ** (jnp.arange(0, RET_D, 2, dtype=F32) / RET_D))
    ang = jnp.arange(t_len, dtype=F32)[:, None] * inv_freq[None, :]
    cos = jnp.cos(ang)
    sin = jnp.sin(ang)
    cos2 = jnp.concatenate([cos, cos], axis=-1)
    sin2 = jnp.concatenate([-sin, sin], axis=-1)
    return cos2, sin2, dmat, bc(qdec), bc(kdec), sdec


def retention(proj, gn_w, *, batch, t_len, blk):
    n = proj.shape[0]
    nt = t_len // blk
    w = RET_HEADS * RET_D
    cos2, sin2, dmat, qdec, kdec, sdec = _retention_tables(t_len, blk)
    gn = gn_w.reshape(RET_HEADS, 1, RET_D)
    col = lambda c: pl.BlockSpec((blk, w), lambda b, t, c=c: (b * nt + t, c))
    tab_t = pl.BlockSpec((blk, RET_D), lambda b, t: (t, 0))
    whole = lambda a: pl.BlockSpec(a.shape, lambda b, t: (0, 0, 0))
    return pl.pallas_call(
        _retention_kernel,
        grid=(batch, nt),
        in_specs=[col(0), col(1), col(2), col(3), tab_t, tab_t,
                  whole(dmat), whole(qdec), whole(kdec), whole(sdec), whole(gn)],
        out_specs=pl.BlockSpec((blk, w), lambda b, t: (b * nt + t, 0)),
        out_shape=jax.ShapeDtypeStruct((n, w), BF16),
        scratch_shapes=[pltpu.VMEM((RET_HEADS, RET_D, RET_D), F32)],
        compiler_params=_cparams(("parallel", "arbitrary")),
        name="retention",
    )(proj, proj, proj, proj, cos2, sin2, dmat, qdec, kdec, sdec, gn)


def _softplus(z):
    return jnp.maximum(z, 0.0) + jnp.log(1.0 + jnp.exp(-jnp.abs(z)))


def _sb_kernel(q_ref, k_ref, v_ref, tri_ref, o_ref, carry_ref, acc_ref, qs_ref, *, tq):
    qi = pl.program_id(1)
    lane = lax.broadcasted_iota(jnp.int32, (1, LANES), 1)
    lo_lanes = lane < SB_DH
    tri2 = tri_ref[...]

    n_heads = 2 * SB_PAIRS
    for p in range(SB_PAIRS):
        q = q_ref[:, p * LANES:(p + 1) * LANES].astype(F32) * (SB_DH ** -0.5)
        qs_ref[(2 * p) * tq:(2 * p + 1) * tq, :] = jnp.where(lo_lanes, q, 0.0).astype(BF16)
        qs_ref[(2 * p + 1) * tq:(2 * p + 2) * tq, :] = jnp.where(lo_lanes, 0.0, q).astype(BF16)

    def tile(kv_start, mask):
        z = jnp.concatenate([
            lax.dot_general(qs_ref[2 * p * tq:(2 * p + 2) * tq, :], k_ref[pl.ds(kv_start, tq), p * LANES:(p + 1) * LANES],
                            (((1,), (1,)), ((), ())), preferred_element_type=F32)
            for p in range(SB_PAIRS)], axis=0)
        sp = _softplus(z)
        if mask is not None:
            mask = jnp.concatenate([mask] * n_heads, axis=0)
            sp = jnp.where(mask, sp, 0.0)
        hi = sp.astype(BF16)
        lo = (sp - hi.astype(F32)).astype(BF16)
        sums = jnp.dot(jnp.concatenate([hi, lo], axis=1), tri2, preferred_element_type=F32)
        carry = carry_ref[...]
        new_carry = carry + jnp.broadcast_to(sums[:, 0:1], carry.shape)
        carry_ref[...] = new_carry
        a = jnp.exp(z - sums - jnp.concatenate([carry] * (tq // LANES), axis=1))
        if mask is not None:
            a = jnp.where(mask, a, 0.0)
        a = a.astype(BF16)
        for p in range(SB_PAIRS):
            cols = slice(p * LANES, (p + 1) * LANES)
            pv = jnp.dot(a[2 * p * tq:(2 * p + 2) * tq, :], v_ref[pl.ds(kv_start, tq), cols],
                         preferred_element_type=F32)
            acc_ref[:, cols] += jnp.where(lo_lanes, pv[0:tq], pv[tq:2 * tq])
        return jnp.min(new_carry)

    carry_ref[...] = jnp.zeros_like(carry_ref)
    acc_ref[...] = jnp.zeros_like(acc_ref)
    row = lax.broadcasted_iota(jnp.int32, (tq, tq), 0)
    colm = lax.broadcasted_iota(jnp.int32, (tq, tq), 1)
    least0 = tile(pl.multiple_of(qi * tq, tq), colm < row)

    def alive(state):
        jj, least = state
        return jnp.logical_and(jj < qi, least < SB_DEAD)

    def body(state):
        jj, _ = state
        return jj + 1, tile(pl.multiple_of((qi - 1 - jj) * tq, tq), None)

    lax.while_loop(alive, body, (jnp.int32(0), least0))
    o_ref[...] = acc_ref[...].astype(o_ref.dtype)


def stick_breaking(proj, *, batch, t_len, tq, col0):
    n = proj.shape[0]
    nq = t_len // tq
    w = SB_PAIRS * LANES
    r = jnp.arange(tq)
    tri = (r[:, None] >= r[None, :]).astype(BF16)
    tri2 = jnp.concatenate([tri, tri], axis=0)
    return pl.pallas_call(
        functools.partial(_sb_kernel, tq=tq),
        grid=(batch, nq),
        in_specs=[pl.BlockSpec((tq, w), lambda b, i: (b * nq + i, col0)),
                  pl.BlockSpec((t_len, w), lambda b, i: (b, col0 + 1)),
                  pl.BlockSpec((t_len, w), lambda b, i: (b, col0 + 2)),
                  pl.BlockSpec(tri2.shape, lambda b, i: (0, 0))],
        out_specs=pl.BlockSpec((tq, w), lambda b, i: (b * nq + i, 0)),
        out_shape=jax.ShapeDtypeStruct((n, w), BF16),
        scratch_shapes=[pltpu.VMEM((2 * SB_PAIRS * tq, LANES), F32), pltpu.VMEM((tq, w), F32),
                        pltpu.VMEM((2 * SB_PAIRS * tq, LANES), BF16)],
        compiler_params=_cparams(("parallel", "arbitrary")),
        name="stick_breaking",
    )(proj, proj, proj, tri2)


def _load_once(pairs, sem):
    @pl.when(pl.program_id(0) == 0)
    def _():
        copies = [pltpu.make_async_copy(src, dst, sem.at[j]) for j, (src, dst) in enumerate(pairs)]
        for c in copies:
            c.start()
        for c in copies:
            c.wait()


def _ffn_kernel(*refs, n_in, fc):
    x_ref, g_ref = refs[0], refs[1]
    y_refs = refs[2:2 + n_in]
    wo_refs = refs[2 + n_in:2 + 2 * n_in]
    wg_hbm, wu_hbm, wd_hbm, o_ref, wg, wu, wd, sem = refs[2 + 2 * n_in:]
    _load_once(((wg_hbm, wg), (wu_hbm, wu), (wd_hbm, wd)), sem)

    x1 = x_ref[...]
    for y_ref, wo_ref in zip(y_refs, wo_refs):
        x1 = x1 + jnp.dot(y_ref[...], wo_ref[...], preferred_element_type=F32)
    h = _rms(x1, g_ref[...]).astype(BF16)
    ff = jnp.zeros_like(x1)
    for c0 in range(0, wg.shape[1], fc):
        c1 = min(c0 + fc, wg.shape[1])
        gate = jnp.dot(h, wg[:, c0:c1], preferred_element_type=F32)
        up = jnp.dot(h, wu[:, c0:c1], preferred_element_type=F32)
        act = (gate * jax.nn.sigmoid(gate) * up).astype(BF16)
        ff = ff + jnp.dot(act, wd[c0:c1, :], preferred_element_type=F32)
    o_ref[...] = x1 + ff


def ffn(x, g, ys, wos, wg, wu, wd, *, tm, fc):
    n, d = x.shape
    f = wg.shape[1]
    n_in = len(ys)
    hbm = pl.BlockSpec(memory_space=pl.ANY)
    in_specs = [pl.BlockSpec((tm, d), lambda i: (i, 0)), pl.BlockSpec((1, d), lambda i: (0, 0))]
    in_specs += [pl.BlockSpec((tm, y.shape[1]), lambda i: (i, 0)) for y in ys]
    in_specs += [pl.BlockSpec(w.shape, lambda i: (0, 0)) for w in wos]
    in_specs += [hbm, hbm, hbm]
    return pl.pallas_call(
        functools.partial(_ffn_kernel, n_in=n_in, fc=fc),
        grid=(n // tm,),
        in_specs=in_specs,
        out_specs=pl.BlockSpec((tm, d), lambda i: (i, 0)),
        out_shape=jax.ShapeDtypeStruct((n, d), F32),
        scratch_shapes=[pltpu.VMEM((d, f), BF16), pltpu.VMEM((d, f), BF16), pltpu.VMEM((f, d), BF16),
                        pltpu.SemaphoreType.DMA((3,))],
        compiler_params=_cparams(("arbitrary",)),
        name="ffn",
    )(x, g.reshape(1, d), *ys, *wos, wg, wu, wd)


def _shift_rows(cur, tail, d, row8):
    rolled = pltpu.roll(cur, d, 0)
    top = jnp.where(row8 < d, pltpu.roll(tail, d, 0), rolled[0:8])
    return jnp.concatenate([top, rolled[8:]], axis=0)


def _lru_kernel(p_ref, w3_ref, w4_ref, b4_ref, wa_ref, ba_ref, wx_ref, bx_ref, lam_ref, o_ref,
                cu_tail, xr_tail, h_prev, *, blk):
    @pl.when(pl.program_id(1) == 0)
    def _():
        cu_tail[...] = jnp.zeros_like(cu_tail)
        xr_tail[...] = jnp.zeros_like(xr_tail)
        h_prev[...] = jnp.zeros_like(h_prev)

    w = CONV_W
    row8 = lax.broadcasted_iota(jnp.int32, (8, w), 0)
    gb = p_ref[:, 0:w].astype(F32)
    cu = p_ref[:, w:2 * w].astype(F32) * p_ref[:, 2 * w:3 * w].astype(F32)
    xr = p_ref[:, 3 * w:4 * w].astype(F32)
    xg = p_ref[:, 4 * w:5 * w].astype(F32)

    ct = cu_tail[...]
    conv3 = (w3_ref[2:3, :] * cu + w3_ref[1:2, :] * _shift_rows(cu, ct, 1, row8)
             + w3_ref[0:1, :] * _shift_rows(cu, ct, 2, row8))
    cu_tail[...] = cu[blk - 8:blk]
    o_ref[:, 0:w] = (gb * conv3).astype(o_ref.dtype)

    xt = xr_tail[...]
    xc = (w4_ref[3:4, :] * xr + w4_ref[2:3, :] * _shift_rows(xr, xt, 1, row8)
          + w4_ref[1:2, :] * _shift_rows(xr, xt, 2, row8)
          + w4_ref[0:1, :] * _shift_rows(xr, xt, 3, row8)) + b4_ref[...]
    xr_tail[...] = xr[blk - 8:blk]

    xcb = xc.astype(BF16)
    r = jax.nn.sigmoid(jnp.dot(xcb, wa_ref[...], preferred_element_type=F32) + ba_ref[...])
    ig = jax.nn.sigmoid(jnp.dot(xcb, wx_ref[...], preferred_element_type=F32) + bx_ref[...])
    log_a = (-LRU_C * r) * _softplus(-lam_ref[...])
    a = jnp.exp(log_a)
    b = jnp.sqrt(1.0 - a * a) * (ig * xc)

    row = lax.broadcasted_iota(jnp.int32, (blk, w), 0)
    d = 1
    while d < blk:
        keep = row >= d
        a_s = jnp.where(keep, pltpu.roll(a, d, 0), 1.0)
        b_s = jnp.where(keep, pltpu.roll(b, d, 0), 0.0)
        b = a * b_s + b
        a = a * a_s
        d *= 2
    h = a * h_prev[...] + b
    h_prev[...] = h[blk - 1:blk]

    c0 = math.sqrt(2.0 / math.pi)
    gelu = 0.5 * xg * (1.0 + jnp.tanh(c0 * (xg + 0.044715 * (xg * xg * xg))))
    o_ref[:, w:2 * w] = (h * gelu).astype(o_ref.dtype)


def _block_diag(wh):
    nh, bi, bo = wh.shape
    eye = jnp.eye(nh, dtype=wh.dtype)
    return (eye[:, None, :, None] * wh[:, :, None, :]).reshape(nh * bi, nh * bo)


def lru_mixer(proj, w3, w4, b4, wa, ba, wx, bx, lam, *, batch, t_len, blk):
    n, pw = proj.shape
    nt = t_len // blk
    w = CONV_W
    row = lambda a: a.reshape(1, w)
    full = lambda shp: pl.BlockSpec(shp, lambda b, t: (0, 0))
    return pl.pallas_call(
        functools.partial(_lru_kernel, blk=blk),
        grid=(batch, nt),
        in_specs=[pl.BlockSpec((blk, pw), lambda b, t: (b * nt + t, 0)),
                  full((8, w)), full((8, w)), full((1, w)),
                  full((w, w)), full((1, w)), full((w, w)), full((1, w)), full((1, w))],
        out_specs=pl.BlockSpec((blk, 2 * w), lambda b, t: (b * nt + t, 0)),
        out_shape=jax.ShapeDtypeStruct((n, 2 * w), BF16),
        scratch_shapes=[pltpu.VMEM((8, w), F32), pltpu.VMEM((8, w), F32), pltpu.VMEM((1, w), F32)],
        compiler_params=_cparams(("parallel", "arbitrary")),
        name="lru_mixer",
    )(proj, jnp.pad(w3, ((0, 8 - w3.shape[0]), (0, 0))), jnp.pad(w4, ((0, 8 - w4.shape[0]), (0, 0))),
      row(b4), _block_diag(wa).astype(BF16), row(ba), _block_diag(wx).astype(BF16), row(bx), row(lam))


def _router_kernel(x_ref, y_ref, wo_ref, g_ref, wh_ref, wl_ref, b_ref, lt_ref, x1_ref, hp_ref, r_ref, rt_ref,
                   c_ref):
    x1 = x_ref[...] + jnp.dot(y_ref[...], wo_ref[...], preferred_element_type=F32)
    x1_ref[...] = x1
    h = _rms(x1, g_ref[...])
    hh = h.astype(BF16)
    hp_ref[...] = hh
    hl = (h - hh.astype(F32)).astype(BF16)
    wh = wh_ref[...]
    logits = (jnp.dot(hh, wh, preferred_element_type=F32) + jnp.dot(hl, wh, preferred_element_type=F32)
              + jnp.dot(hh, wl_ref[...], preferred_element_type=F32)) + b_ref[...]
    lane = lax.broadcasted_iota(jnp.int32, logits.shape, 1).astype(F32)
    neg = jnp.float32(-jnp.inf)
    logits = jnp.where(lane < N_EXPERTS, logits, neg)
    m0 = jnp.max(logits, axis=-1, keepdims=True)
    i0 = jnp.min(jnp.where(logits == m0, lane, float(LANES)), axis=-1, keepdims=True)
    rest = jnp.where(lane == i0, neg, logits)
    m1 = jnp.max(rest, axis=-1, keepdims=True)
    i1 = jnp.min(jnp.where(rest == m1, lane, float(LANES)), axis=-1, keepdims=True)
    e1 = jnp.exp(m1 - m0)
    w0 = 1.0 / (1.0 + e1)
    w1 = e1 * w0
    oh0 = jnp.where(lane == i0, 1.0, 0.0)
    oh1 = jnp.where(lane == i1, 1.0, 0.0)
    oh = oh0 + oh1
    before = jnp.dot(lt_ref[...], oh.astype(BF16), preferred_element_type=F32)
    rank0 = jnp.sum(before * oh0, axis=-1, keepdims=True)
    rank1 = jnp.sum(before * oh1, axis=-1, keepdims=True)
    out = jnp.where(lane == 0, i0, 0.0)
    out = jnp.where(lane == 1, i1, out)
    out = jnp.where(lane == 2, w0, out)
    out = jnp.where(lane == 3, w1, out)
    out = jnp.where(lane == 4, rank0, out)
    out = jnp.where(lane == 5, rank1, out)
    r_ref[...] = out
    rt_ref[...] = out.T[0:8, :]
    c_ref[0] = jnp.sum(oh, axis=0, keepdims=True)


def router(x, y, wo, g, w_router, b_router, *, tm):
    n, d = x.shape
    nt = n // tm
    wp = jnp.pad(w_router, ((0, 0), (0, LANES - N_EXPERTS)))
    wh = wp.astype(BF16)
    wl = (wp - wh.astype(F32)).astype(BF16)
    bp = jnp.pad(b_router, (0, LANES - N_EXPERTS)).reshape(1, LANES)
    t = jnp.arange(tm)
    lower = (t[None, :] < t[:, None]).astype(BF16)
    const = lambda shp: pl.BlockSpec(shp, lambda i: (0, 0))
    return pl.pallas_call(
        _router_kernel,
        grid=(nt,),
        in_specs=[pl.BlockSpec((tm, d), lambda i: (i, 0)), pl.BlockSpec((tm, y.shape[1]), lambda i: (i, 0)),
                  const(wo.shape), const((1, d)), const((d, LANES)),
                  const((d, LANES)), const((1, LANES)), const((tm, tm))],
        out_specs=[pl.BlockSpec((tm, d), lambda i: (i, 0)), pl.BlockSpec((tm, d), lambda i: (i, 0)),
                   pl.BlockSpec((tm, LANES), lambda i: (i, 0)), pl.BlockSpec((None, 8, tm), lambda i: (i, 0, 0)),
                   pl.BlockSpec((1, 1, LANES), lambda i: (i, 0, 0))],
        out_shape=[jax.ShapeDtypeStruct((n, d), F32), jax.ShapeDtypeStruct((n, d), BF16),
                   jax.ShapeDtypeStruct((n, LANES), F32), jax.ShapeDtypeStruct((nt, 8, tm), F32),
                   jax.ShapeDtypeStruct((nt, 1, LANES), F32)],
        compiler_params=_cparams(("parallel",)),
        name="router",
    )(x, y, wo, g.reshape(1, d), wh, wl, bp, lower)


GRAN = 8


def _run_rows(tm):
    return -(-(2 * tm + N_EXPERTS * GRAN) // LANES) * LANES


def _slot_plan(counts, *, tm, tb):
    nt = counts.shape[0]
    n = nt * tm
    cnt = counts[:, 0, :N_EXPERTS].astype(jnp.int32)
    run = (cnt + GRAN - 1) // GRAN * GRAN
    local = jnp.cumsum(run, axis=1) - run
    total = jnp.sum(run, axis=0)
    padded = (total + tb - 1) // tb * tb
    pad_end = jnp.cumsum(padded)
    first = (pad_end - padded)[None, :] + jnp.cumsum(run, axis=0) - run
    local_mat = jnp.broadcast_to(jnp.pad(local, ((0, 0), (0, LANES - N_EXPERTS)))[:, :, None],
                                 (nt, LANES, LANES)).astype(BF16)
    n_slots = -(-(2 * n + nt * N_EXPERTS * GRAN + N_EXPERTS * tb) // tb) * tb
    blk_start = jnp.arange(n_slots // tb, dtype=jnp.int32) * tb
    block_e = jnp.minimum(jnp.sum(blk_start[:, None] >= pad_end[None, :], axis=1), N_EXPERTS - 1)
    n_used = (pad_end[N_EXPERTS - 1] // tb).reshape(1)
    tables = tuple(a.reshape(-1).astype(jnp.int32) for a in (first, run // GRAN, local))
    return (tables, local_mat, block_e.astype(jnp.int32), n_used.astype(jnp.int32),
            pad_end.astype(jnp.int32), n_slots)


def _run_positions(r, local_mat):
    lane = lax.broadcasted_iota(jnp.int32, r.shape, 1).astype(F32)

    def pos(col_e, col_rank):
        onehot = jnp.where(lane == r[:, col_e:col_e + 1], 1.0, 0.0).astype(BF16)
        return jnp.dot(onehot, local_mat, preferred_element_type=F32) + r[:, col_rank:col_rank + 1]
    return pos(0, 4), pos(1, 5)


def _run_copies(tables, step, make_copy, fn, *, tm):
    first_ref, pieces_ref, local_ref = tables
    top_bit = (tm // GRAN).bit_length() - 1
    for e in range(N_EXPERTS):
        k = step * N_EXPERTS + e
        first = first_ref[k]
        local = local_ref[k]
        pieces = pieces_ref[k]
        for bit in range(top_bit, -1, -1):
            before = lax.shift_left(lax.shift_right_logical(pieces, bit + 1), bit + 1) * GRAN

            def one(first=first, local=local, before=before, bit=bit):
                fn(make_copy(pl.multiple_of(first + before, GRAN), pl.multiple_of(local + before, GRAN),
                             GRAN << bit))
            pl.when(lax.shift_right_logical(pieces, bit) & 1 == 1)(one)


def _dispatch_kernel(first_ref, pieces_ref, local_ref, pe_ref, h_ref, rt_ref, xb_hbm, cbuf, zbuf,
                     sem_r, sem_z, *, tb):
    i = pl.program_id(0)
    n = pl.num_programs(0)
    slot = i % 2
    tables = (first_ref, pieces_ref, local_ref)
    d = h_ref.shape[1]

    def copies(step, s, fn):
        _run_copies(tables, step, lambda row, loc, rows: pltpu.make_async_copy(
            cbuf.at[s, pl.ds(loc, rows)], xb_hbm.at[pl.ds(row, rows)], sem_r.at[s]), fn, tm=h_ref.shape[0])

    def zero_block(first):
        cp = pltpu.make_async_copy(zbuf, xb_hbm.at[pl.ds(pl.multiple_of(first, tb), tb)], sem_z.at[0])
        cp.start()
        cp.wait()

    @pl.when(i == 0)
    def _():
        zbuf[...] = jnp.zeros_like(zbuf)
        for e in range(N_EXPERTS):
            end = pe_ref[e]
            start = pe_ref[e - 1] if e > 0 else 0
            pl.when(end > start)(functools.partial(zero_block, end - tb))
            tail = pe_ref[N_EXPERTS - 1] + e * tb
            pl.when(tail < xb_hbm.shape[0])(functools.partial(zero_block, tail))

    def step(s):
        @pl.when(i >= 2)
        def _():
            copies(i - 2, s, lambda c: c.wait())

        cr, tm = cbuf.shape[1], h_ref.shape[0]
        e0, e1, w0, w1 = rt_ref[0:1, :], rt_ref[1:2, :], rt_ref[2:3, :], rt_ref[3:4, :]
        pos0, pos1 = rt_ref[4:5, :], rt_ref[5:6, :]
        for e in range(N_EXPERTS):
            start = local_ref[i * N_EXPERTS + e].astype(F32)
            pos0 = pos0 + jnp.where(e0 == e, start, 0.0)
            pos1 = pos1 + jnp.where(e1 == e, start, 0.0)
        j = lax.broadcasted_iota(jnp.int32, (cr, tm), 0).astype(F32)
        m0 = j == pos0
        m1 = j == pos1
        pick = jnp.where(jnp.logical_or(m0, m1), 1.0, 0.0).astype(BF16)
        cbuf[s, :, 0:d] = jnp.dot(pick, h_ref[...], preferred_element_type=F32)
        wsel = jnp.sum(jnp.where(m0, w0, 0.0) + jnp.where(m1, w1, 0.0), axis=1, keepdims=True)
        cbuf[s, :, d:d + LANES] = jnp.broadcast_to(wsel, (cr, LANES))
        copies(i, s, lambda c: c.start())

        @pl.when(i == n - 1)
        def _():
            copies(i, s, lambda c: c.wait())

            @pl.when(n > 1)
            def _():
                copies(i - 1, 1 - s, lambda c: c.wait())

    _for_slot(slot, step)


def dispatch(hp, r_t, tables, pad_end, n_slots, *, tm, tb):
    n, d = hp.shape
    cr = _run_rows(tm)
    grid_spec = pltpu.PrefetchScalarGridSpec(
        num_scalar_prefetch=4,
        grid=(n // tm,),
        in_specs=[pl.BlockSpec((tm, d), lambda i, *_: (i, 0)),
                  pl.BlockSpec((None, 8, tm), lambda i, *_: (i, 0, 0))],
        out_specs=pl.BlockSpec(memory_space=pl.ANY),
        scratch_shapes=[pltpu.VMEM((2, cr, d + LANES), F32), pltpu.VMEM((tb, d + LANES), F32),
                        pltpu.SemaphoreType.DMA((2,)), pltpu.SemaphoreType.DMA((1,))],
    )
    return pl.pallas_call(
        functools.partial(_dispatch_kernel, tb=tb),
        grid_spec=grid_spec,
        out_shape=jax.ShapeDtypeStruct((n_slots, d + LANES), F32),
        compiler_params=_cparams(("arbitrary",)),
        name="dispatch",
    )(*tables, pad_end, hp, r_t)


def _moe_kernel(be_ref, nu_ref, x_ref, wg_hbm, wu_hbm, wd_hbm, o_ref, wg, wu, wd, sg, su, sd, sem, *, fc):
    i = pl.program_id(0)
    n_used = nu_ref[0]
    d = o_ref.shape[1]
    n_chunks = wg.shape[1] // fc

    def staged(e, c):
        s = c % 2
        cols = pl.ds(c * fc, fc)
        return (pltpu.make_async_copy(wg_hbm.at[e, :, cols], sg.at[s], sem.at[3 * s]),
                pltpu.make_async_copy(wu_hbm.at[e, :, cols], su.at[s], sem.at[3 * s + 1]),
                pltpu.make_async_copy(wd_hbm.at[e, cols, :], sd.at[s], sem.at[3 * s + 2]))

    def fetch(e, c):
        for cp in staged(e, c):
            cp.start()

    def install(e, c):
        for cp in staged(e, c):
            cp.wait()
        s = c % 2
        wg[:, c * fc:(c + 1) * fc] = sg[s].astype(BF16)
        wu[:, c * fc:(c + 1) * fc] = su[s].astype(BF16)
        wd[c * fc:(c + 1) * fc, :] = sd[s].astype(BF16)
        if c + 2 < n_chunks:
            fetch(e, c + 2)

    def compute(next_e):
        x = x_ref[:, 0:d].astype(BF16)
        acc = jnp.zeros(o_ref.shape, F32)
        for c in range(n_chunks):
            gate = jnp.dot(x, wg[:, c * fc:(c + 1) * fc], preferred_element_type=F32)
            up = jnp.dot(x, wu[:, c * fc:(c + 1) * fc], preferred_element_type=F32)
            act = (gate * jax.nn.sigmoid(gate) * up).astype(BF16)
            acc = acc + jnp.dot(act, wd[c * fc:(c + 1) * fc, :], preferred_element_type=F32)
            if next_e is not None:
                install(next_e, c)
        o_ref[...] = acc * x_ref[:, d:d + 1]

    @pl.when(i == 0)
    def _():
        e = be_ref[0]
        fetch(e, 0)
        fetch(e, 1)
        for c in range(n_chunks):
            install(e, c)

    is_last = jnp.logical_and(i + 1 < n_used, be_ref[jnp.minimum(i + 1, pl.num_programs(0) - 1)] != be_ref[i])

    @pl.when(jnp.logical_and(i < n_used, is_last))
    def _():
        next_e = be_ref[i + 1]
        fetch(next_e, 0)
        fetch(next_e, 1)
        compute(next_e)

    @pl.when(jnp.logical_and(i < n_used, jnp.logical_not(is_last)))
    def _():
        compute(None)

    @pl.when(i >= n_used)
    def _():
        o_ref[...] = jnp.zeros_like(o_ref)


def moe_experts(xb, block_e, n_used, wg, wu, wd, *, tb, fc):
    n_slots, dp = xb.shape
    d = dp - LANES
    f = wg.shape[2]
    blk = lambda i, be, nu: (jnp.maximum(jnp.minimum(i, nu[0] - 1), 0), 0)
    grid_spec = pltpu.PrefetchScalarGridSpec(
        num_scalar_prefetch=2,
        grid=(n_slots // tb,),
        in_specs=[pl.BlockSpec((tb, dp), blk),
                  pl.BlockSpec(memory_space=pl.ANY), pl.BlockSpec(memory_space=pl.ANY),
                  pl.BlockSpec(memory_space=pl.ANY)],
        out_specs=pl.BlockSpec((tb, d), lambda i, be, nu: (i, 0)),
        scratch_shapes=[pltpu.VMEM((d, f), BF16), pltpu.VMEM((d, f), BF16), pltpu.VMEM((f, d), BF16),
                        pltpu.VMEM((2, d, fc), F32), pltpu.VMEM((2, d, fc), F32), pltpu.VMEM((2, fc, d), F32),
                        pltpu.SemaphoreType.DMA((6,))],
    )
    return pl.pallas_call(
        functools.partial(_moe_kernel, fc=fc),
        grid_spec=grid_spec,
        out_shape=jax.ShapeDtypeStruct((n_slots, d), F32),
        compiler_params=_cparams(("arbitrary",)),
        name="moe_experts",
    )(block_e, n_used, xb, wg, wu, wd)


def _combine_kernel(first_ref, pieces_ref, local_ref, x_ref, r_ref, lmat_ref, g_ref, yb_hbm, o_ref, ybuf, sem_g):
    i = pl.program_id(0)
    n = pl.num_programs(0)
    slot = i % 2
    tables = (first_ref, pieces_ref, local_ref)

    def copies(step, s, fn):
        _run_copies(tables, step, lambda row, loc, rows: pltpu.make_async_copy(
            yb_hbm.at[pl.ds(row, rows)], ybuf.at[s, pl.ds(loc, rows)], sem_g.at[s]), fn, tm=x_ref.shape[0])

    @pl.when(i == 0)
    def _():
        ybuf[...] = jnp.zeros_like(ybuf)
        copies(0, 0, lambda c: c.start())

    def step(s):
        @pl.when(i + 1 < n)
        def _():
            copies(i + 1, 1 - s, lambda c: c.start())

        copies(i, s, lambda c: c.wait())
        tm, cr = x_ref.shape[0], ybuf.shape[1]
        p0, p1 = (jnp.concatenate([p] * (cr // LANES), axis=1) for p in _run_positions(r_ref[...], lmat_ref[...]))
        j = lax.broadcasted_iota(jnp.int32, (tm, cr), 1).astype(F32)
        pick = jnp.where(jnp.logical_or(j == p0, j == p1), 1.0, 0.0).astype(BF16)
        y = jnp.dot(pick, ybuf[s].astype(BF16), preferred_element_type=F32)
        o_ref[...] = _rms(x_ref[...] + y, g_ref[...])

    _for_slot(slot, step)


def combine_final(x, r, local_mat, yb, tables, g, *, tm):
    n, d = x.shape
    grid_spec = pltpu.PrefetchScalarGridSpec(
        num_scalar_prefetch=3,
        grid=(n // tm,),
        in_specs=[pl.BlockSpec((tm, d), lambda i, *_: (i, 0)), pl.BlockSpec((tm, LANES), lambda i, *_: (i, 0)),
                  pl.BlockSpec((None, LANES, LANES), lambda i, *_: (i, 0, 0)),
                  pl.BlockSpec((1, d), lambda i, *_: (0, 0)), pl.BlockSpec(memory_space=pl.ANY)],
        out_specs=pl.BlockSpec((tm, d), lambda i, *_: (i, 0)),
        scratch_shapes=[pltpu.VMEM((2, _run_rows(tm), d), F32), pltpu.SemaphoreType.DMA((2,))],
    )
    return pl.pallas_call(
        _combine_kernel,
        grid_spec=grid_spec,
        out_shape=jax.ShapeDtypeStruct((n, d), F32),
        compiler_params=_cparams(("arbitrary",)),
        name="combine_final",
    )(*tables, x, r, local_mat, g.reshape(1, d), yb)


def even_layer(x, g_mix, g_ffn, w_in, gn_w, w_out, wg, wu, wd, *, batch, t_len):
    proj = norm_proj(x, g_mix, w_in.astype(BF16), tm=512)
    y_ret = retention(proj, gn_w, batch=batch, t_len=t_len, blk=256)
    y_sb = stick_breaking(proj, batch=batch, t_len=t_len, tq=256, col0=4)
    half = y_ret.shape[1]
    wo = w_out.astype(BF16)
    return ffn(x, g_ffn, [y_ret, y_sb], [wo[:half], wo[half:]],
               wg.astype(BF16), wu.astype(BF16), wd.astype(BF16), tm=512, fc=256)


def odd_layer_mixer(x, g_mix, w_in, conv_w, lru_conv_w, lru_conv_b, wa, ba, wx, bx, lam, *, batch, t_len):
    proj = norm_proj(x, g_mix, w_in.astype(BF16), tm=512)
    return lru_mixer(proj, conv_w, lru_conv_w, lru_conv_b, wa, ba, wx, bx, lam,
                     batch=batch, t_len=t_len, blk=256)


def moe_and_final(x, y, w_out, g_ffn, g_final, w_router, b_router, wg, wu, wd, *, tm=512, tb=512):
    x, hp, r, r_t, counts = router(x, y, w_out.astype(BF16), g_ffn, w_router, b_router, tm=tm)
    tables, local_mat, block_e, n_used, pad_end, n_slots = _slot_plan(counts, tm=tm, tb=tb)
    xb = dispatch(hp, r_t, tables, pad_end, n_slots, tm=tm, tb=tb)
    yb = moe_experts(xb, block_e, n_used, wg, wu, wd, tb=tb, fc=512)
    return combine_final(x, r, local_mat, yb, tables, g_final, tm=tm)


def kernel(x, norm_mix, norm_ffn, norm_final, ev_w_in, ev_ret_gn, ev_w_out, ev_ffn_gate, ev_ffn_up,
           ev_ffn_down, od_w_in, od_conv_w, od_lru_conv_w, od_lru_conv_b, od_lru_wa, od_lru_ba,
           od_lru_wx, od_lru_bx, od_lru_lambda, od_w_out, od_router_w, od_router_b, od_exp_gate,
           od_exp_up, od_exp_down):
    batch, t_len, d = x.shape
    xf = x.reshape(batch * t_len, d)
    xf = even_layer(xf, norm_mix[0], norm_ffn[0], ev_w_in[0], ev_ret_gn[0], ev_w_out[0],
                    ev_ffn_gate[0], ev_ffn_up[0], ev_ffn_down[0], batch=batch, t_len=t_len)
    y = odd_layer_mixer(xf, norm_mix[1], od_w_in[0], od_conv_w[0], od_lru_conv_w[0], od_lru_conv_b[0],
                        od_lru_wa[0], od_lru_ba[0], od_lru_wx[0], od_lru_bx[0], od_lru_lambda[0],
                        batch=batch, t_len=t_len)
    out = moe_and_final(xf, y, od_w_out[0], norm_ffn[1], norm_final, od_router_w[0], od_router_b[0],
                        od_exp_gate[0], od_exp_up[0], od_exp_down[0])
    return out.reshape(batch, t_len, d)
```

```python
import functools
import math

import jax
import jax.numpy as jnp
from jax import lax
from jax.experimental import pallas as pl
from jax.experimental.pallas import tpu as pltpu

F32 = jnp.float32
BF16 = jnp.bfloat16

EPS = 1e-6
CHUNK = 64
RET_HEADS = 4
RET_D = 128
SB_PAIRS = 4
SB_DH = 64
ROPE_BASE = 10000.0
CONV_W = 512
LRU_W = 512
LRU_C = 8.0
N_EXPERTS = 8
LANES = 128
VMEM_LIMIT = 56 * 1024 * 1024
SB_DEAD = 104.0


def _cparams(sem):
    return pltpu.CompilerParams(dimension_semantics=sem, vmem_limit_bytes=VMEM_LIMIT)


def _rms(x, g):
    return x * lax.rsqrt(jnp.mean(x * x, axis=-1, keepdims=True) + EPS) * g


def _for_slot(slot, fn):
    for s in range(2):
        pl.when(slot == s)(functools.partial(fn, s))


def _norm_proj_kernel(x_ref, g_ref, w_ref, o_ref):
    h = _rms(x_ref[...], g_ref[...]).astype(BF16)
    o_ref[...] = jnp.dot(h, w_ref[...], preferred_element_type=F32).astype(o_ref.dtype)


def norm_proj(x, g, w, *, tm):
    n, d = x.shape
    m = w.shape[1]
    return pl.pallas_call(
        _norm_proj_kernel,
        grid=(n // tm,),
        in_specs=[
            pl.BlockSpec((tm, d), lambda i: (i, 0)),
            pl.BlockSpec((1, d), lambda i: (0, 0)),
            pl.BlockSpec((d, m), lambda i: (0, 0)),
        ],
        out_specs=pl.BlockSpec((tm, m), lambda i: (i, 0)),
        out_shape=jax.ShapeDtypeStruct((n, m), BF16),
        compiler_params=_cparams(("parallel",)),
        name="norm_proj",
    )(x, g.reshape(1, d), w)


def _retention_kernel(q_ref, k_ref, v_ref, g_ref, cos_ref, sin_ref, dmat_ref, qdec_ref, kdec_ref,
                      sdec_ref, gn_ref, o_ref, s_ref):
    @pl.when(pl.program_id(1) == 0)
    def _():
        s_ref[...] = jnp.zeros_like(s_ref)

    cos = cos_ref[...]
    sin = sin_ref[...]
    half = RET_D // 2
    for h in range(RET_HEADS):
        cols = slice(h * RET_D, (h + 1) * RET_D)
        q = q_ref[:, cols].astype(F32)
        k = k_ref[:, cols].astype(F32)
        q = q * cos + pltpu.roll(q, half, 1) * sin
        k = (k * cos + pltpu.roll(k, half, 1) * sin) * (RET_D ** -0.5)
        qb = q.astype(BF16)
        kb = k.astype(BF16)
        v = v_ref[:, cols]

        s = lax.dot_general(qb, kb, (((1,), (1,)), ((), ())), preferred_element_type=F32)
        s = s * dmat_ref[h]
        o = jnp.dot(s.astype(BF16), v, preferred_element_type=F32)
        state = s_ref[h]
        o = o + jnp.dot(qb, state.astype(BF16), preferred_element_type=F32) * qdec_ref[h]

        kd = (k * kdec_ref[h]).astype(BF16)
        kv = lax.dot_general(kd, v, (((0,), (0,)), ((), ())), preferred_element_type=F32)
        s_ref[h] = state * sdec_ref[h] + kv

        mu = jnp.mean(o, axis=-1, keepdims=True)
        oc = o - mu
        var = jnp.mean(oc * oc, axis=-1, keepdims=True)
        on = oc * lax.rsqrt(var + EPS)
        gate = g_ref[:, cols].astype(F32)
        o_ref[:, cols] = (on * gn_ref[h] * (gate * jax.nn.sigmoid(gate))).astype(o_ref.dtype)


def _retention_tables(t_len, blk):
    h = jnp.arange(RET_HEADS, dtype=F32)
    log_g = jnp.log(1.0 - 2.0 ** (-5.0 - h))
    idx = jnp.arange(blk, dtype=F32)
    diff = idx[:, None] - idx[None, :]
    ci = jnp.arange(blk)[:, None] // CHUNK
    cj = jnp.arange(blk)[None, :] // CHUNK
    expo = jnp.where(ci == cj, jnp.abs(diff), diff)
    dmat = jnp.where(cj <= ci, jnp.exp(log_g[:, None, None] * expo[None]), 0.0)
    qdec = jnp.exp(log_g[:, None] * (idx + 1.0)[None, :])
    kdec = jnp.exp(log_g[:, None] * (blk - 1.0 - idx)[None, :])
    sdec = jnp.exp(log_g * blk)
    bc = lambda a: jnp.broadcast_to(a[:, :, None], (RET_HEADS, blk, RET_D))
    sdec = jnp.broadcast_to(sdec[:, None, None], (RET_HEADS, 1, RET_D))
    inv_freq = 1.0 / (ROPE_BASE ** (jnp.arange(0, RET_D, 2, dtype=F32) / RET_D))
    ang = jnp.arange(t_len, dtype=F32)[:, None] * inv_freq[None, :]
    cos = jnp.cos(ang)
    sin = jnp.sin(ang)
    cos2 = jnp.concatenate([cos, cos], axis=-1)
    sin2 = jnp.concatenate([-sin, sin], axis=-1)
    return cos2, sin2, dmat, bc(qdec), bc(kdec), sdec


def retention(proj, gn_w, *, batch, t_len, blk):
    n = proj.shape[0]
    nt = t_len // blk
    w = RET_HEADS * RET_D
    cos2, sin2, dmat, qdec, kdec, sdec = _retention_tables(t_len, blk)
    gn = gn_w.reshape(RET_HEADS, 1, RET_D)
    col = lambda c: pl.BlockSpec((blk, w), lambda b, t, c=c: (b * nt + t, c))
    tab_t = pl.BlockSpec((blk, RET_D), lambda b, t: (t, 0))
    whole = lambda a: pl.BlockSpec(a.shape, lambda b, t: (0, 0, 0))
    return pl.pallas_call(
        _retention_kernel,
        grid=(batch, nt),
        in_specs=[col(0), col(1), col(2), col(3), tab_t, tab_t,
                  whole(dmat), whole(qdec), whole(kdec), whole(sdec), whole(gn)],
        out_specs=pl.BlockSpec((blk, w), lambda b, t: (b * nt + t, 0)),
        out_shape=jax.ShapeDtypeStruct((n, w), BF16),
        scratch_shapes=[pltpu.VMEM((RET_HEADS, RET_D, RET_D), F32)],
        compiler_params=_cparams(("parallel", "arbitrary")),
        name="retention",
    )(proj, proj, proj, proj, cos2, sin2, dmat, qdec, kdec, sdec, gn)


def _softplus(z):
    return jnp.maximum(z, 0.0) + jnp.log(1.0 + jnp.exp(-jnp.abs(z)))


def _sb_kernel(q_ref, k_ref, v_ref, tri_ref, o_ref, carry_ref, acc_ref, qs_ref, *, tq):
    qi = pl.program_id(1)
    lane = lax.broadcasted_iota(jnp.int32, (1, LANES), 1)
    lo_lanes = lane < SB_DH
    tri2 = tri_ref[...]

    n_heads = 2 * SB_PAIRS
    for p in range(SB_PAIRS):
        q = q_ref[:, p * LANES:(p + 1) * LANES].astype(F32) * (SB_DH ** -0.5)
        qs_ref[(2 * p) * tq:(2 * p + 1) * tq, :] = jnp.where(lo_lanes, q, 0.0).astype(BF16)
        qs_ref[(2 * p + 1) * tq:(2 * p + 2) * tq, :] = jnp.where(lo_lanes, 0.0, q).astype(BF16)

    def tile(kv_start, mask):
        z = jnp.concatenate([
            lax.dot_general(qs_ref[2 * p * tq:(2 * p + 2) * tq, :], k_ref[pl.ds(kv_start, tq), p * LANES:(p + 1) * LANES],
                            (((1,), (1,)), ((), ())), preferred_element_type=F32)
            for p in range(SB_PAIRS)], axis=0)
        sp = _softplus(z)
        if mask is not None:
            mask = jnp.concatenate([mask] * n_heads, axis=0)
            sp = jnp.where(mask, sp, 0.0)
        hi = sp.astype(BF16)
        lo = (sp - hi.astype(F32)).astype(BF16)
        sums = jnp.dot(jnp.concatenate([hi, lo], axis=1), tri2, preferred_element_type=F32)
        carry = carry_ref[...]
        new_carry = carry + jnp.broadcast_to(sums[:, 0:1], carry.shape)
        carry_ref[...] = new_carry
        a = jnp.exp(z - sums - jnp.concatenate([carry] * (tq // LANES), axis=1))
        if mask is not None:
            a = jnp.where(mask, a, 0.0)
        a = a.astype(BF16)
        for p in range(SB_PAIRS):
            cols = slice(p * LANES, (p + 1) * LANES)
            pv = jnp.dot(a[2 * p * tq:(2 * p + 2) * tq, :], v_ref[pl.ds(kv_start, tq), cols],
                         preferred_element_type=F32)
            acc_ref[:, cols] += jnp.where(lo_lanes, pv[0:tq], pv[tq:2 * tq])
        return jnp.min(new_carry)

    carry_ref[...] = jnp.zeros_like(carry_ref)
    acc_ref[...] = jnp.zeros_like(acc_ref)
    row = lax.broadcasted_iota(jnp.int32, (tq, tq), 0)
    colm = lax.broadcasted_iota(jnp.int32, (tq, tq), 1)
    def diagonal_only():
        return tile(pl.multiple_of(qi * tq, tq), colm < row)

    def diagonal_and_previous():
        tile(pl.multiple_of(qi * tq, tq), colm < row)
        return tile(pl.multiple_of((qi - 1) * tq, tq), None)

    least0 = lax.cond(qi > 0, diagonal_and_previous, diagonal_only)

    def alive(state):
        jj, least = state
        return jnp.logical_and(jj < qi, least < SB_DEAD)

    def body(state):
        jj, _ = state
        return jj + 1, tile(pl.multiple_of((qi - 1 - jj) * tq, tq), None)

    lax.while_loop(alive, body, (jnp.int32(1), least0))
    o_ref[...] = acc_ref[...].astype(o_ref.dtype)


def stick_breaking(proj, *, batch, t_len, tq, col0):
    n = proj.shape[0]
    nq = t_len // tq
    w = SB_PAIRS * LANES
    r = jnp.arange(tq)
    tri = (r[:, None] >= r[None, :]).astype(BF16)
    tri2 = jnp.concatenate([tri, tri], axis=0)
    return pl.pallas_call(
        functools.partial(_sb_kernel, tq=tq),
        grid=(batch, nq),
        in_specs=[pl.BlockSpec((tq, w), lambda b, i: (b * nq + i, col0)),
                  pl.BlockSpec((t_len, w), lambda b, i: (b, col0 + 1)),
                  pl.BlockSpec((t_len, w), lambda b, i: (b, col0 + 2)),
                  pl.BlockSpec(tri2.shape, lambda b, i: (0, 0))],
        out_specs=pl.BlockSpec((tq, w), lambda b, i: (b * nq + i, 0)),
        out_shape=jax.ShapeDtypeStruct((n, w), BF16),
        scratch_shapes=[pltpu.VMEM((2 * SB_PAIRS * tq, LANES), F32), pltpu.VMEM((tq, w), F32),
                        pltpu.VMEM((2 * SB_PAIRS * tq, LANES), BF16)],
        compiler_params=_cparams(("parallel", "arbitrary")),
        name="stick_breaking",
    )(proj, proj, proj, tri2)


def _load_once(pairs, sem):
    @pl.when(pl.program_id(0) == 0)
    def _():
        copies = [pltpu.make_async_copy(src, dst, sem.at[j]) for j, (src, dst) in enumerate(pairs)]
        for c in copies:
            c.start()
        for c in copies:
            c.wait()


def _ffn_kernel(*refs, n_in, fc):
    x_ref, g_ref = refs[0], refs[1]
    y_refs = refs[2:2 + n_in]
    wo_refs = refs[2 + n_in:2 + 2 * n_in]
    wg_hbm, wu_hbm, wd_hbm, o_ref, wg, wu, wd, sem = refs[2 + 2 * n_in:]
    _load_once(((wg_hbm, wg), (wu_hbm, wu), (wd_hbm, wd)), sem)

    x1 = x_ref[...]
    for y_ref, wo_ref in zip(y_refs, wo_refs):
        x1 = x1 + jnp.dot(y_ref[...], wo_ref[...], preferred_element_type=F32)
    h = _rms(x1, g_ref[...]).astype(BF16)
    ff = jnp.zeros_like(x1)
    for c0 in range(0, wg.shape[1], fc):
        c1 = min(c0 + fc, wg.shape[1])
        gate = jnp.dot(h, wg[:, c0:c1], preferred_element_type=F32)
        up = jnp.dot(h, wu[:, c0:c1], preferred_element_type=F32)
        act = (gate * jax.nn.sigmoid(gate) * up).astype(BF16)
        ff = ff + jnp.dot(act, wd[c0:c1, :], preferred_element_type=F32)
    o_ref[...] = x1 + ff


def ffn(x, g, ys, wos, wg, wu, wd, *, tm, fc):
    n, d = x.shape
    f = wg.shape[1]
    n_in = len(ys)
    hbm = pl.BlockSpec(memory_space=pl.ANY)
    in_specs = [pl.BlockSpec((tm, d), lambda i: (i, 0)), pl.BlockSpec((1, d), lambda i: (0, 0))]
    in_specs += [pl.BlockSpec((tm, y.shape[1]), lambda i: (i, 0)) for y in ys]
    in_specs += [pl.BlockSpec(w.shape, lambda i: (0, 0)) for w in wos]
    in_specs += [hbm, hbm, hbm]
    return pl.pallas_call(
        functools.partial(_ffn_kernel, n_in=n_in, fc=fc),
        grid=(n // tm,),
        in_specs=in_specs,
        out_specs=pl.BlockSpec((tm, d), lambda i: (i, 0)),
        out_shape=jax.ShapeDtypeStruct((n, d), F32),
        scratch_shapes=[pltpu.VMEM((d, f), BF16), pltpu.VMEM((d, f), BF16), pltpu.VMEM((f, d), BF16),
                        pltpu.SemaphoreType.DMA((3,))],
        compiler_params=_cparams(("arbitrary",)),
        name="ffn",
    )(x, g.reshape(1, d), *ys, *wos, wg, wu, wd)


def _shift_rows(cur, tail, d, row8):
    rolled = pltpu.roll(cur, d, 0)
    top = jnp.where(row8 < d, pltpu.roll(tail, d, 0), rolled[0:8])
    return jnp.concatenate([top, rolled[8:]], axis=0)


def _lru_kernel(p_ref, w3_ref, w4_ref, b4_ref, wa_ref, ba_ref, wx_ref, bx_ref, lam_ref, o_ref,
                cu_tail, xr_tail, h_prev, *, blk):
    @pl.when(pl.program_id(1) == 0)
    def _():
        cu_tail[...] = jnp.zeros_like(cu_tail)
        xr_tail[...] = jnp.zeros_like(xr_tail)
        h_prev[...] = jnp.zeros_like(h_prev)

    w = CONV_W
    row8 = lax.broadcasted_iota(jnp.int32, (8, w), 0)
    gb = p_ref[:, 0:w].astype(F32)
    cu = p_ref[:, w:2 * w].astype(F32) * p_ref[:, 2 * w:3 * w].astype(F32)
    xr = p_ref[:, 3 * w:4 * w].astype(F32)
    xg = p_ref[:, 4 * w:5 * w].astype(F32)

    ct = cu_tail[...]
    conv3 = (w3_ref[2:3, :] * cu + w3_ref[1:2, :] * _shift_rows(cu, ct, 1, row8)
             + w3_ref[0:1, :] * _shift_rows(cu, ct, 2, row8))
    cu_tail[...] = cu[blk - 8:blk]
    o_ref[:, 0:w] = (gb * conv3).astype(o_ref.dtype)

    xt = xr_tail[...]
    xc = (w4_ref[3:4, :] * xr + w4_ref[2:3, :] * _shift_rows(xr, xt, 1, row8)
          + w4_ref[1:2, :] * _shift_rows(xr, xt, 2, row8)
          + w4_ref[0:1, :] * _shift_rows(xr, xt, 3, row8)) + b4_ref[...]
    xr_tail[...] = xr[blk - 8:blk]

    xcb = xc.astype(BF16)
    r = jax.nn.sigmoid(jnp.dot(xcb, wa_ref[...], preferred_element_type=F32) + ba_ref[...])
    ig = jax.nn.sigmoid(jnp.dot(xcb, wx_ref[...], preferred_element_type=F32) + bx_ref[...])
    log_a = (-LRU_C * r) * _softplus(-lam_ref[...])
    a = jnp.exp(log_a)
    b = jnp.sqrt(1.0 - a * a) * (ig * xc)

    row = lax.broadcasted_iota(jnp.int32, (blk, w), 0)
    d = 1
    while d < blk:
        keep = row >= d
        a_s = jnp.where(keep, pltpu.roll(a, d, 0), 1.0)
        b_s = jnp.where(keep, pltpu.roll(b, d, 0), 0.0)
        b = a * b_s + b
        a = a * a_s
        d *= 2
    h = a * h_prev[...] + b
    h_prev[...] = h[blk - 1:blk]

    c0 = math.sqrt(2.0 / math.pi)
    gelu = 0.5 * xg * (1.0 + jnp.tanh(c0 * (xg + 0.044715 * (xg * xg * xg))))
    o_ref[:, w:2 * w] = (h * gelu).astype(o_ref.dtype)


def _block_diag(wh):
    nh, bi, bo = wh.shape
    eye = jnp.eye(nh, dtype=wh.dtype)
    return (eye[:, None, :, None] * wh[:, :, None, :]).reshape(nh * bi, nh * bo)


def lru_mixer(proj, w3, w4, b4, wa, ba, wx, bx, lam, *, batch, t_len, blk):
    n, pw = proj.shape
    nt = t_len // blk
    w = CONV_W
    row = lambda a: a.reshape(1, w)
    full = lambda shp: pl.BlockSpec(shp, lambda b, t: (0, 0))
    return pl.pallas_call(
        functools.partial(_lru_kernel, blk=blk),
        grid=(batch, nt),
        in_specs=[pl.BlockSpec((blk, pw), lambda b, t: (b * nt + t, 0)),
                  full((8, w)), full((8, w)), full((1, w)),
                  full((w, w)), full((1, w)), full((w, w)), full((1, w)), full((1, w))],
        out_specs=pl.BlockSpec((blk, 2 * w), lambda b, t: (b * nt + t, 0)),
        out_shape=jax.ShapeDtypeStruct((n, 2 * w), BF16),
        scratch_shapes=[pltpu.VMEM((8, w), F32), pltpu.VMEM((8, w), F32), pltpu.VMEM((1, w), F32)],
        compiler_params=_cparams(("parallel", "arbitrary")),
        name="lru_mixer",
    )(proj, jnp.pad(w3, ((0, 8 - w3.shape[0]), (0, 0))), jnp.pad(w4, ((0, 8 - w4.shape[0]), (0, 0))),
      row(b4), _block_diag(wa).astype(BF16), row(ba), _block_diag(wx).astype(BF16), row(bx), row(lam))


def _router_kernel(x_ref, y_ref, wo_ref, g_ref, wh_ref, wl_ref, b_ref, lt_ref, x1_ref, hp_ref, r_ref, rt_ref,
                   c_ref):
    x1 = x_ref[...] + jnp.dot(y_ref[...], wo_ref[...], preferred_element_type=F32)
    x1_ref[...] = x1
    h = _rms(x1, g_ref[...])
    hh = h.astype(BF16)
    hp_ref[...] = hh
    hl = (h - hh.astype(F32)).astype(BF16)
    wh = wh_ref[...]
    logits = (jnp.dot(hh, wh, preferred_element_type=F32) + jnp.dot(hl, wh, preferred_element_type=F32)
              + jnp.dot(hh, wl_ref[...], preferred_element_type=F32)) + b_ref[...]
    lane = lax.broadcasted_iota(jnp.int32, logits.shape, 1).astype(F32)
    neg = jnp.float32(-jnp.inf)
    logits = jnp.where(lane < N_EXPERTS, logits, neg)
    m0 = jnp.max(logits, axis=-1, keepdims=True)
    i0 = jnp.min(jnp.where(logits == m0, lane, float(LANES)), axis=-1, keepdims=True)
    rest = jnp.where(lane == i0, neg, logits)
    m1 = jnp.max(rest, axis=-1, keepdims=True)
    i1 = jnp.min(jnp.where(rest == m1, lane, float(LANES)), axis=-1, keepdims=True)
    e1 = jnp.exp(m1 - m0)
    w0 = 1.0 / (1.0 + e1)
    w1 = e1 * w0
    oh0 = jnp.where(lane == i0, 1.0, 0.0)
    oh1 = jnp.where(lane == i1, 1.0, 0.0)
    oh = oh0 + oh1
    before = jnp.dot(lt_ref[...], oh.astype(BF16), preferred_element_type=F32)
    rank0 = jnp.sum(before * oh0, axis=-1, keepdims=True)
    rank1 = jnp.sum(before * oh1, axis=-1, keepdims=True)
    out = jnp.where(lane == 0, i0, 0.0)
    out = jnp.where(lane == 1, i1, out)
    out = jnp.where(lane == 2, w0, out)
    out = jnp.where(lane == 3, w1, out)
    out = jnp.where(lane == 4, rank0, out)
    out = jnp.where(lane == 5, rank1, out)
    r_ref[...] = out
    rt_ref[...] = out.T[0:8, :]
    c_ref[0] = jnp.sum(oh, axis=0, keepdims=True)


def router(x, y, wo, g, w_router, b_router, *, tm):
    n, d = x.shape
    nt = n // tm
    wp = jnp.pad(w_router, ((0, 0), (0, LANES - N_EXPERTS)))
    wh = wp.astype(BF16)
    wl = (wp - wh.astype(F32)).astype(BF16)
    bp = jnp.pad(b_router, (0, LANES - N_EXPERTS)).reshape(1, LANES)
    t = jnp.arange(tm)
    lower = (t[None, :] < t[:, None]).astype(BF16)
    const = lambda shp: pl.BlockSpec(shp, lambda i: (0, 0))
    return pl.pallas_call(
        _router_kernel,
        grid=(nt,),
        in_specs=[pl.BlockSpec((tm, d), lambda i: (i, 0)), pl.BlockSpec((tm, y.shape[1]), lambda i: (i, 0)),
                  const(wo.shape), const((1, d)), const((d, LANES)),
                  const((d, LANES)), const((1, LANES)), const((tm, tm))],
        out_specs=[pl.BlockSpec((tm, d), lambda i: (i, 0)), pl.BlockSpec((tm, d), lambda i: (i, 0)),
                   pl.BlockSpec((tm, LANES), lambda i: (i, 0)), pl.BlockSpec((None, 8, tm), lambda i: (i, 0, 0)),
                   pl.BlockSpec((1, 1, LANES), lambda i: (i, 0, 0))],
        out_shape=[jax.ShapeDtypeStruct((n, d), F32), jax.ShapeDtypeStruct((n, d), BF16),
                   jax.ShapeDtypeStruct((n, LANES), F32), jax.ShapeDtypeStruct((nt, 8, tm), F32),
                   jax.ShapeDtypeStruct((nt, 1, LANES), F32)],
        compiler_params=_cparams(("parallel",)),
        name="router",
    )(x, y, wo, g.reshape(1, d), wh, wl, bp, lower)


GRAN = 8


def _run_rows(tm):
    return -(-(2 * tm + N_EXPERTS * GRAN) // LANES) * LANES


def _slot_plan(counts, *, tm, tb):
    nt = counts.shape[0]
    n = nt * tm
    cnt = counts[:, 0, :N_EXPERTS].astype(jnp.int32)
    run = (cnt + GRAN - 1) // GRAN * GRAN
    local = jnp.cumsum(run, axis=1) - run
    total = jnp.sum(run, axis=0)
    padded = (total + tb - 1) // tb * tb
    pad_end = jnp.cumsum(padded)
    first = (pad_end - padded)[None, :] + jnp.cumsum(run, axis=0) - run
    local_mat = jnp.broadcast_to(jnp.pad(local, ((0, 0), (0, LANES - N_EXPERTS)))[:, :, None],
                                 (nt, LANES, LANES)).astype(BF16)
    n_slots = -(-(2 * n + nt * N_EXPERTS * GRAN + N_EXPERTS * tb) // tb) * tb
    blk_start = jnp.arange(n_slots // tb, dtype=jnp.int32) * tb
    block_e = jnp.minimum(jnp.sum(blk_start[:, None] >= pad_end[None, :], axis=1), N_EXPERTS - 1)
    n_used = (pad_end[N_EXPERTS - 1] // tb).reshape(1)
    tables = tuple(a.reshape(-1).astype(jnp.int32) for a in (first, run // GRAN, local))
    return (tables, local_mat, block_e.astype(jnp.int32), n_used.astype(jnp.int32),
            pad_end.astype(jnp.int32), n_slots)


def _run_positions(r, local_mat):
    lane = lax.broadcasted_iota(jnp.int32, r.shape, 1).astype(F32)

    def pos(col_e, col_rank):
        onehot = jnp.where(lane == r[:, col_e:col_e + 1], 1.0, 0.0).astype(BF16)
        return jnp.dot(onehot, local_mat, preferred_element_type=F32) + r[:, col_rank:col_rank + 1]
    return pos(0, 4), pos(1, 5)


def _run_copies(tables, step, make_copy, fn, *, tm):
    first_ref, pieces_ref, local_ref = tables
    top_bit = (tm // GRAN).bit_length() - 1
    for e in range(N_EXPERTS):
        k = step * N_EXPERTS + e
        first = first_ref[k]
        local = local_ref[k]
        pieces = pieces_ref[k]
        for bit in range(top_bit, -1, -1):
            before = lax.shift_left(lax.shift_right_logical(pieces, bit + 1), bit + 1) * GRAN

            def one(first=first, local=local, before=before, bit=bit):
                fn(make_copy(pl.multiple_of(first + before, GRAN), pl.multiple_of(local + before, GRAN),
                             GRAN << bit))
            pl.when(lax.shift_right_logical(pieces, bit) & 1 == 1)(one)


def _dispatch_kernel(first_ref, pieces_ref, local_ref, pe_ref, h_ref, rt_ref, xb_hbm, cbuf, zbuf,
                     sem_r, sem_z, *, tb):
    i = pl.program_id(0)
    n = pl.num_programs(0)
    slot = i % 2
    tables = (first_ref, pieces_ref, local_ref)
    d = h_ref.shape[1]

    def copies(step, s, fn):
        _run_copies(tables, step, lambda row, loc, rows: pltpu.make_async_copy(
            cbuf.at[s, pl.ds(loc, rows)], xb_hbm.at[pl.ds(row, rows)], sem_r.at[s]), fn, tm=h_ref.shape[0])

    def zero_block(first):
        cp = pltpu.make_async_copy(zbuf, xb_hbm.at[pl.ds(pl.multiple_of(first, tb), tb)], sem_z.at[0])
        cp.start()
        cp.wait()

    @pl.when(i == 0)
    def _():
        zbuf[...] = jnp.zeros_like(zbuf)
        for e in range(N_EXPERTS):
            end = pe_ref[e]
            start = pe_ref[e - 1] if e > 0 else 0
            pl.when(end > start)(functools.partial(zero_block, end - tb))
            tail = pe_ref[N_EXPERTS - 1] + e * tb
            pl.when(tail < xb_hbm.shape[0])(functools.partial(zero_block, tail))

    def step(s):
        @pl.when(i >= 2)
        def _():
            copies(i - 2, s, lambda c: c.wait())

        cr, tm = cbuf.shape[1], h_ref.shape[0]
        e0, e1, w0, w1 = rt_ref[0:1, :], rt_ref[1:2, :], rt_ref[2:3, :], rt_ref[3:4, :]
        pos0, pos1 = rt_ref[4:5, :], rt_ref[5:6, :]
        for e in range(N_EXPERTS):
            start = local_ref[i * N_EXPERTS + e].astype(F32)
            pos0 = pos0 + jnp.where(e0 == e, start, 0.0)
            pos1 = pos1 + jnp.where(e1 == e, start, 0.0)
        j = lax.broadcasted_iota(jnp.int32, (cr, tm), 0).astype(F32)
        m0 = j == pos0
        m1 = j == pos1
        pick = jnp.where(jnp.logical_or(m0, m1), 1.0, 0.0).astype(BF16)
        cbuf[s, :, 0:d] = jnp.dot(pick, h_ref[...], preferred_element_type=F32)
        wsel = jnp.sum(jnp.where(m0, w0, 0.0) + jnp.where(m1, w1, 0.0), axis=1, keepdims=True)
        cbuf[s, :, d:d + LANES] = jnp.broadcast_to(wsel, (cr, LANES))
        copies(i, s, lambda c: c.start())

        @pl.when(i == n - 1)
        def _():
            copies(i, s, lambda c: c.wait())

            @pl.when(n > 1)
            def _():
                copies(i - 1, 1 - s, lambda c: c.wait())

    _for_slot(slot, step)


def dispatch(hp, r_t, tables, pad_end, n_slots, *, tm, tb):
    n, d = hp.shape
    cr = _run_rows(tm)
    grid_spec = pltpu.PrefetchScalarGridSpec(
        num_scalar_prefetch=4,
        grid=(n // tm,),
        in_specs=[pl.BlockSpec((tm, d), lambda i, *_: (i, 0)),
                  pl.BlockSpec((None, 8, tm), lambda i, *_: (i, 0, 0))],
        out_specs=pl.BlockSpec(memory_space=pl.ANY),
        scratch_shapes=[pltpu.VMEM((2, cr, d + LANES), F32), pltpu.VMEM((tb, d + LANES), F32),
                        pltpu.SemaphoreType.DMA((2,)), pltpu.SemaphoreType.DMA((1,))],
    )
    return pl.pallas_call(
        functools.partial(_dispatch_kernel, tb=tb),
        grid_spec=grid_spec,
        out_shape=jax.ShapeDtypeStruct((n_slots, d + LANES), F32),
        compiler_params=_cparams(("arbitrary",)),
        name="dispatch",
    )(*tables, pad_end, hp, r_t)


def _moe_kernel(be_ref, nu_ref, x_ref, wg_hbm, wu_hbm, wd_hbm, o_ref, wg, wu, wd, sg, su, sd, sem, *, fc):
    i = pl.program_id(0)
    n_used = nu_ref[0]
    d = o_ref.shape[1]
    n_chunks = wg.shape[1] // fc

    def staged(e, c):
        s = c % 2
        cols = pl.ds(c * fc, fc)
        return (pltpu.make_async_copy(wg_hbm.at[e, :, cols], sg.at[s], sem.at[3 * s]),
                pltpu.make_async_copy(wu_hbm.at[e, :, cols], su.at[s], sem.at[3 * s + 1]),
                pltpu.make_async_copy(wd_hbm.at[e, cols, :], sd.at[s], sem.at[3 * s + 2]))

    def fetch(e, c):
        for cp in staged(e, c):
            cp.start()

    def install(e, c):
        for cp in staged(e, c):
            cp.wait()
        s = c % 2
        wg[:, c * fc:(c + 1) * fc] = sg[s].astype(BF16)
        wu[:, c * fc:(c + 1) * fc] = su[s].astype(BF16)
        wd[c * fc:(c + 1) * fc, :] = sd[s].astype(BF16)
        if c + 2 < n_chunks:
            fetch(e, c + 2)

    def compute(next_e):
        x = x_ref[:, 0:d].astype(BF16)
        acc = jnp.zeros(o_ref.shape, F32)
        for c in range(n_chunks):
            gate = jnp.dot(x, wg[:, c * fc:(c + 1) * fc], preferred_element_type=F32)
            up = jnp.dot(x, wu[:, c * fc:(c + 1) * fc], preferred_element_type=F32)
            act = (gate * jax.nn.sigmoid(gate) * up).astype(BF16)
            acc = acc + jnp.dot(act, wd[c * fc:(c + 1) * fc, :], preferred_element_type=F32)
            if next_e is not None:
                install(next_e, c)
        o_ref[...] = acc * x_ref[:, d:d + 1]

    @pl.when(i == 0)
    def _():
        e = be_ref[0]
        fetch(e, 0)
        fetch(e, 1)
        for c in range(n_chunks):
            install(e, c)

    is_last = jnp.logical_and(i + 1 < n_used, be_ref[jnp.minimum(i + 1, pl.num_programs(0) - 1)] != be_ref[i])

    @pl.when(jnp.logical_and(i < n_used, is_last))
    def _():
        next_e = be_ref[i + 1]
        fetch(next_e, 0)
        fetch(next_e, 1)
        compute(next_e)

    @pl.when(jnp.logical_and(i < n_used, jnp.logical_not(is_last)))
    def _():
        compute(None)

    @pl.when(i >= n_used)
    def _():
        o_ref[...] = jnp.zeros_like(o_ref)


def moe_experts(xb, block_e, n_used, wg, wu, wd, *, tb, fc):
    n_slots, dp = xb.shape
    d = dp - LANES
    f = wg.shape[2]
    blk = lambda i, be, nu: (jnp.maximum(jnp.minimum(i, nu[0] - 1), 0), 0)
    grid_spec = pltpu.PrefetchScalarGridSpec(
        num_scalar_prefetch=2,
        grid=(n_slots // tb,),
        in_specs=[pl.BlockSpec((tb, dp), blk),
                  pl.BlockSpec(memory_space=pl.ANY), pl.BlockSpec(memory_space=pl.ANY),
                  pl.BlockSpec(memory_space=pl.ANY)],
        out_specs=pl.BlockSpec((tb, d), lambda i, be, nu: (i, 0)),
        scratch_shapes=[pltpu.VMEM((d, f), BF16), pltpu.VMEM((d, f), BF16), pltpu.VMEM((f, d), BF16),
                        pltpu.VMEM((2, d, fc), F32), pltpu.VMEM((2, d, fc), F32), pltpu.VMEM((2, fc, d), F32),
                        pltpu.SemaphoreType.DMA((6,))],
    )
    return pl.pallas_call(
        functools.partial(_moe_kernel, fc=fc),
        grid_spec=grid_spec,
        out_shape=jax.ShapeDtypeStruct((n_slots, d), F32),
        compiler_params=_cparams(("arbitrary",)),
        name="moe_experts",
    )(block_e, n_used, xb, wg, wu, wd)


def _combine_kernel(first_ref, pieces_ref, local_ref, x_ref, r_ref, lmat_ref, g_ref, yb_hbm, o_ref, ybuf, sem_g):
    i = pl.program_id(0)
    n = pl.num_programs(0)
    slot = i % 2
    tables = (first_ref, pieces_ref, local_ref)

    def copies(step, s, fn):
        _run_copies(tables, step, lambda row, loc, rows: pltpu.make_async_copy(
            yb_hbm.at[pl.ds(row, rows)], ybuf.at[s, pl.ds(loc, rows)], sem_g.at[s]), fn, tm=x_ref.shape[0])

    @pl.when(i == 0)
    def _():
        ybuf[...] = jnp.zeros_like(ybuf)
        copies(0, 0, lambda c: c.start())

    def step(s):
        @pl.when(i + 1 < n)
        def _():
            copies(i + 1, 1 - s, lambda c: c.start())

        copies(i, s, lambda c: c.wait())
        tm, cr = x_ref.shape[0], ybuf.shape[1]
        p0, p1 = (jnp.concatenate([p] * (cr // LANES), axis=1) for p in _run_positions(r_ref[...], lmat_ref[...]))
        j = lax.broadcasted_iota(jnp.int32, (tm, cr), 1).astype(F32)
        pick = jnp.where(jnp.logical_or(j == p0, j == p1), 1.0, 0.0).astype(BF16)
        y = jnp.dot(pick, ybuf[s].astype(BF16), preferred_element_type=F32)
        o_ref[...] = _rms(x_ref[...] + y, g_ref[...])

    _for_slot(slot, step)


def combine_final(x, r, local_mat, yb, tables, g, *, tm):
    n, d = x.shape
    grid_spec = pltpu.PrefetchScalarGridSpec(
        num_scalar_prefetch=3,
        grid=(n // tm,),
        in_specs=[pl.BlockSpec((tm, d), lambda i, *_: (i, 0)), pl.BlockSpec((tm, LANES), lambda i, *_: (i, 0)),
                  pl.BlockSpec((None, LANES, LANES), lambda i, *_: (i, 0, 0)),
                  pl.BlockSpec((1, d), lambda i, *_: (0, 0)), pl.BlockSpec(memory_space=pl.ANY)],
        out_specs=pl.BlockSpec((tm, d), lambda i, *_: (i, 0)),
        scratch_shapes=[pltpu.VMEM((2, _run_rows(tm), d), F32), pltpu.SemaphoreType.DMA((2,))],
    )
    return pl.pallas_call(
        _combine_kernel,
        grid_spec=grid_spec,
        out_shape=jax.ShapeDtypeStruct((n, d), F32),
        compiler_params=_cparams(("arbitrary",)),
        name="combine_final",
    )(*tables, x, r, local_mat, g.reshape(1, d), yb)


def even_layer(x, g_mix, g_ffn, w_in, gn_w, w_out, wg, wu, wd, *, batch, t_len):
    proj = norm_proj(x, g_mix, w_in.astype(BF16), tm=512)
    y_ret = retention(proj, gn_w, batch=batch, t_len=t_len, blk=256)
    y_sb = stick_breaking(proj, batch=batch, t_len=t_len, tq=256, col0=4)
    half = y_ret.shape[1]
    wo = w_out.astype(BF16)
    return ffn(x, g_ffn, [y_ret, y_sb], [wo[:half], wo[half:]],
               wg.astype(BF16), wu.astype(BF16), wd.astype(BF16), tm=512, fc=256)


def odd_layer_mixer(x, g_mix, w_in, conv_w, lru_conv_w, lru_conv_b, wa, ba, wx, bx, lam, *, batch, t_len):
    proj = norm_proj(x, g_mix, w_in.astype(BF16), tm=512)
    return lru_mixer(proj, conv_w, lru_conv_w, lru_conv_b, wa, ba, wx, bx, lam,
                     batch=batch, t_len=t_len, blk=256)


def moe_and_final(x, y, w_out, g_ffn, g_final, w_router, b_router, wg, wu, wd, *, tm=512, tb=512):
    x, hp, r, r_t, counts = router(x, y, w_out.astype(BF16), g_ffn, w_router, b_router, tm=tm)
    tables, local_mat, block_e, n_used, pad_end, n_slots = _slot_plan(counts, tm=tm, tb=tb)
    xb = dispatch(hp, r_t, tables, pad_end, n_slots, tm=tm, tb=tb)
    yb = moe_experts(xb, block_e, n_used, wg, wu, wd, tb=tb, fc=512)
    return combine_final(x, r, local_mat, yb, tables, g_final, tm=tm)


def kernel(x, norm_mix, norm_ffn, norm_final, ev_w_in, ev_ret_gn, ev_w_out, ev_ffn_gate, ev_ffn_up,
           ev_ffn_down, od_w_in, od_conv_w, od_lru_conv_w, od_lru_conv_b, od_lru_wa, od_lru_ba,
           od_lru_wx, od_lru_bx, od_lru_lambda, od_w_out, od_router_w, od_router_b, od_exp_gate,
           od_exp_up, od_exp_down):
    batch, t_len, d = x.shape
    xf = x.reshape(batch * t_len, d)
    xf = even_layer(xf, norm_mix[0], norm_ffn[0], ev_w_in[0], ev_ret_gn[0], ev_w_out[0],
                    ev_ffn_gate[0], ev_ffn_up[0], ev_ffn_down[0], batch=batch, t_len=t_len)
    y = odd_layer_mixer(xf, norm_mix[1], od_w_in[0], od_conv_w[0], od_lru_conv_w[0], od_lru_conv_b[0],
                        od_lru_wa[0], od_lru_ba[0], od_lru_wx[0], od_lru_bx[0], od_lru_lambda[0],
                        batch=batch, t_len=t_len)
    out = moe_and_final(xf, y, od_w_out[0], norm_ffn[1], norm_final, od_router_w[0], od_router_b[0],
                        od_exp_gate[0], od_exp_up[0], od_exp_down[0])
    return out.reshape(batch, t_len, d)
```
